```python
import math
import jax, jax.numpy as jnp
from jax import lax
import numpy as np

D_MODEL = 1024
BATCH = 8
SEQ = 8192
DEPTH = 1
DEC_BATCH = 128
DEC_SEQ = 1
PAST_LEN = 8192
PAGE_SIZE = 128

MIX_WIDTH = D_MODEL
ATTN_WIDTH = MIX_WIDTH // 2
CONV_WIDTH = MIX_WIDTH - ATTN_WIDTH
N_HEADS = 4
D_HEAD = ATTN_WIDTH // (2 * N_HEADS)
QK_DIM = 2 * D_HEAD
V_DIM = 2 * D_HEAD
CONV_K = 31
CONV_STATE = CONV_K - 1
D_FF = -(-8 * D_MODEL // (3 * 256)) * 256
N_ADA = 6
QBLK = 128
EPS = 1e-6
NEG = -1e30
ATTN_SCALE = 1.0 / math.sqrt(D_HEAD)
Q_COLS = N_HEADS * QK_DIM
K_COLS = N_HEADS * QK_DIM
V_COLS = N_HEADS * V_DIM
GLU_COLS = 2 * CONV_WIDTH
IN_COLS = Q_COLS + K_COLS + V_COLS + GLU_COLS

kernel_name = 'hymba_style_diffattn_conformer_conv_decode_step'


def rmsnorm(x, g):
    x32 = x.astype(jnp.float32)
    y = x32 * lax.rsqrt(jnp.mean(x32 * x32, axis=-1, keepdims=True) + EPS)
    return (y * g.astype(jnp.float32)).astype(x.dtype)


def layernorm(x, g, b):
    x32 = x.astype(jnp.float32)
    mu = jnp.mean(x32, axis=-1, keepdims=True)
    xc = x32 - mu
    y = xc * lax.rsqrt(jnp.mean(xc * xc, axis=-1, keepdims=True) + EPS)
    return (y * g.astype(jnp.float32) + b.astype(jnp.float32)).astype(x.dtype)


def ada_mod(c, w_ada, b_ada):
    m = jax.nn.silu(c) @ w_ada + b_ada
    return jnp.split(m[:, None, :], N_ADA, axis=-1)


def in_proj(h, w_in, b_glu):
    b, t = h.shape[:2]
    z = h @ w_in
    q = z[..., :Q_COLS].reshape(b, t, N_HEADS, 2, D_HEAD)
    k = z[..., Q_COLS:Q_COLS + K_COLS].reshape(b, t, N_HEADS, 2, D_HEAD)
    v = z[..., Q_COLS + K_COLS:Q_COLS + K_COLS + V_COLS].reshape(b, t, N_HEADS, V_DIM)
    ga = z[..., Q_COLS + K_COLS + V_COLS:] + b_glu
    u = ga[..., :CONV_WIDTH] * jax.nn.sigmoid(ga[..., CONV_WIDTH:])
    return q, k, v, u


def conv_branch(u_ext, conv_w, conv_b, ln_g, ln_b):
    y = lax.conv_general_dilated(
        u_ext, conv_w[:, None, :].astype(u_ext.dtype), window_strides=(1,), padding='VALID',
        dimension_numbers=('NWC', 'WIO', 'NWC'), feature_group_count=CONV_WIDTH)
    return jax.nn.silu(layernorm(y + conv_b, ln_g, ln_b))


def diff_lambda(lq1, lk1, lq2, lk2, lam_init):
    f = jnp.float32
    return (jnp.exp(jnp.sum(lq1.astype(f) * lk1.astype(f))) -
            jnp.exp(jnp.sum(lq2.astype(f) * lk2.astype(f))) + lam_init)


def diff_attn_prompt(q, k, v, lam):
    b, s = q.shape[:2]
    nb = s // QBLK
    qb = jnp.moveaxis(q.reshape(b, nb, QBLK, N_HEADS, 2, D_HEAD), 1, 0)
    kpos = jnp.arange(s)

    def one_block(args):
        qi, bi = args
        sc = jnp.einsum('bqhcd,bkhcd->bhcqk', qi, k, preferred_element_type=jnp.float32) * ATTN_SCALE
        qpos = bi * QBLK + jnp.arange(QBLK)
        mask = kpos[None, :] <= qpos[:, None]
        p = jax.nn.softmax(jnp.where(mask, sc, NEG), axis=-1)
        a = p[:, :, 0] - lam * p[:, :, 1]
        return jnp.einsum('bhqk,bkhd->bqhd', a.astype(v.dtype), v, preferred_element_type=jnp.float32)

    o = lax.map(one_block, (qb, jnp.arange(nb)))
    return jnp.moveaxis(o, 0, 1).reshape(b, s, N_HEADS, V_DIM)


def diff_attn_sample(q, k_new, v_new, k_past, v_past, lam):
    t = q.shape[1]
    n_past = k_past.shape[1]
    s_past = jnp.einsum('bqhcd,bkhcd->bhcqk', q, k_past, preferred_element_type=jnp.float32) * ATTN_SCALE
    s_new = jnp.einsum('bqhcd,bkhcd->bhcqk', q, k_new, preferred_element_type=jnp.float32) * ATTN_SCALE
    causal = jnp.arange(t)[None, :] <= jnp.arange(t)[:, None]
    s_new = jnp.where(causal, s_new, NEG)
    p = jax.nn.softmax(jnp.concatenate([s_past, s_new], axis=-1), axis=-1)
    a = p[:, :, 0] - lam * p[:, :, 1]
    o = jnp.einsum('bhqk,bkhd->bqhd', a[..., :n_past].astype(v_past.dtype), v_past,
                   preferred_element_type=jnp.float32)
    o = o + jnp.einsum('bhqk,bkhd->bqhd', a[..., n_past:].astype(v_new.dtype), v_new,
                       preferred_element_type=jnp.float32)
    return o


def attn_heads_out(o, subln_g, lam_init, dtype):
    b, t = o.shape[:2]
    y = rmsnorm(o, subln_g) * (1.0 - lam_init)
    return y.reshape(b, t, ATTN_WIDTH).astype(dtype)


def swiglu(h, w_gate, w_up, w_down):
    return (jax.nn.silu(h @ w_gate) * (h @ w_up)) @ w_down


def setup_inputs(seed: int = 0) -> dict:
    key = jax.random.key(seed)
    ks = jax.random.split(key, 40)
    n_pages = PAST_LEN // PAGE_SIZE
    n_used = DEC_BATCH * n_pages
    n_pool = n_used + max(1, n_used // 4)
    f = jnp.float32
    nrm = lambda k, shape, s=1.0: jax.random.normal(k, shape, f) * s
    gain = lambda k, shape: 1.0 + 0.02 * jax.random.normal(k, shape, f)
    page_table = jax.random.permutation(ks[7], n_pool)[:n_used].reshape(DEC_BATCH, n_pages).astype(jnp.int32)
    return {
        'x_prompt': nrm(ks[0], (BATCH, SEQ, D_MODEL)),
        'x_sample': nrm(ks[1], (DEC_BATCH, DEC_SEQ, D_MODEL)),
        'c_prompt': nrm(ks[2], (BATCH, D_MODEL)),
        'c_sample': nrm(ks[3], (DEC_BATCH, D_MODEL)),
        'cache_k': nrm(ks[4], (DEPTH, n_pool, PAGE_SIZE, N_HEADS, QK_DIM)),
        'cache_v': nrm(ks[5], (DEPTH, n_pool, PAGE_SIZE, N_HEADS, V_DIM)),
        'state_conv': nrm(ks[6], (DEPTH, DEC_BATCH, CONV_STATE, CONV_WIDTH)),
        'page_table': page_table,
        'w_ada': nrm(ks[8], (DEPTH, D_MODEL, N_ADA * D_MODEL), D_MODEL ** -0.5),
        'b_ada': nrm(ks[9], (DEPTH, N_ADA * D_MODEL), 0.02),
        'g_pre_mix': gain(ks[10], (DEPTH, D_MODEL)),
        'g_post_mix': gain(ks[11], (DEPTH, D_MODEL)),
        'g_pre_ffn': gain(ks[12], (DEPTH, D_MODEL)),
        'g_post_ffn': gain(ks[13], (DEPTH, D_MODEL)),
        'w_in': nrm(ks[14], (DEPTH, D_MODEL, IN_COLS), D_MODEL ** -0.5),
        'b_glu': nrm(ks[15], (DEPTH, GLU_COLS), 0.02),
        'conv_w': nrm(ks[16], (DEPTH, CONV_K, CONV_WIDTH), CONV_K ** -0.5),
        'conv_b': nrm(ks[17], (DEPTH, CONV_WIDTH), 0.02),
        'conv_ln_g': gain(ks[18], (DEPTH, CONV_WIDTH)),
        'conv_ln_b': nrm(ks[19], (DEPTH, CONV_WIDTH), 0.02),
        'lambda_q1': nrm(ks[20], (DEPTH, D_HEAD), 0.1),
        'lambda_k1': nrm(ks[21], (DEPTH, D_HEAD), 0.1),
        'lambda_q2': nrm(ks[22], (DEPTH, D_HEAD), 0.1),
        'lambda_k2': nrm(ks[23], (DEPTH, D_HEAD), 0.1),
        'subln_g': gain(ks[24], (DEPTH, V_DIM)),
        'w_out': nrm(ks[25], (DEPTH, MIX_WIDTH, D_MODEL), MIX_WIDTH ** -0.5),
        'w_gate': nrm(ks[26], (DEPTH, D_MODEL, D_FF), D_MODEL ** -0.5),
        'w_up': nrm(ks[27], (DEPTH, D_MODEL, D_FF), D_MODEL ** -0.5),
        'w_down': nrm(ks[28], (DEPTH, D_FF, D_MODEL), D_FF ** -0.5),
    }


def reference(x_prompt, x_sample, c_prompt, c_sample, cache_k, cache_v, state_conv, page_table,
              w_ada, b_ada, g_pre_mix, g_post_mix, g_pre_ffn, g_post_ffn, w_in, b_glu,
              conv_w, conv_b, conv_ln_g, conv_ln_b, lambda_q1, lambda_k1, lambda_q2, lambda_k2,
              subln_g, w_out, w_gate, w_up, w_down):
    xp, xs = x_prompt, x_sample
    kp_l, vp_l, cp_l, ks_l, vs_l, cs_l = [], [], [], [], [], []
    db, n_pages = page_table.shape
    n_past = n_pages * cache_k.shape[2]
    for l in range(DEPTH):
        lam_init = 0.8 - 0.6 * math.exp(-0.3 * l)
        lam = diff_lambda(lambda_q1[l], lambda_k1[l], lambda_q2[l], lambda_k2[l], lam_init)

        def mixer_prompt(h):
            q, k, v, u = in_proj(h, w_in[l], b_glu[l])
            u_ext = jnp.pad(u, ((0, 0), (CONV_STATE, 0), (0, 0)))
            yc = conv_branch(u_ext, conv_w[l], conv_b[l], conv_ln_g[l], conv_ln_b[l])
            o = diff_attn_prompt(q, k, v, lam)
            ya = attn_heads_out(o, subln_g[l], lam_init, h.dtype)
            m = jnp.concatenate([ya, yc], axis=-1) @ w_out[l]
            b, t = h.shape[:2]
            return m, k.reshape(b, t, N_HEADS, QK_DIM), v, u_ext[:, -CONV_STATE:]

        def mixer_sample(h):
            q, k, v, u = in_proj(h, w_in[l], b_glu[l])
            u_ext = jnp.concatenate([state_conv[l].astype(u.dtype), u], axis=1)
            yc = conv_branch(u_ext, conv_w[l], conv_b[l], conv_ln_g[l], conv_ln_b[l])
            k_past = cache_k[l, page_table].reshape(db, n_past, N_HEADS, 2, D_HEAD)
            v_past = cache_v[l, page_table].reshape(db, n_past, N_HEADS, V_DIM)
            o = diff_attn_sample(q, k, v, k_past.astype(q.dtype), v_past.astype(v.dtype), lam)
            ya = attn_heads_out(o, subln_g[l], lam_init, h.dtype)
            m = jnp.concatenate([ya, yc], axis=-1) @ w_out[l]
            b, t = h.shape[:2]
            return m, k.reshape(b, t, N_HEADS, QK_DIM), v, u_ext[:, -CONV_STATE:]

        def layer(x, c, mixer):
            sh1, sc1, gt1, sh2, sc2, gt2 = ada_mod(c, w_ada[l], b_ada[l])
            h = rmsnorm(x, g_pre_mix[l]) * (1.0 + sc1) + sh1
            m, k_new, v_new, conv_new = mixer(h)
            x = x + gt1 * rmsnorm(m, g_post_mix[l])
            h = rmsnorm(x, g_pre_ffn[l]) * (1.0 + sc2) + sh2
            x = x + gt2 * rmsnorm(swiglu(h, w_gate[l], w_up[l], w_down[l]), g_post_ffn[l])
            return x, k_new, v_new, conv_new

        xp, kn, vn, cn = layer(xp, c_prompt, mixer_prompt)
        kp_l.append(kn); vp_l.append(vn); cp_l.append(cn)
        xs, kn, vn, cn = layer(xs, c_sample, mixer_sample)
        ks_l.append(kn); vs_l.append(vn); cs_l.append(cn)

    k_prompt = jnp.stack(kp_l)
    v_prompt = jnp.stack(vp_l)
    conv_prompt = jnp.stack(cp_l)
    k_sample = jnp.stack(ks_l)
    v_sample = jnp.stack(vs_l)
    conv_sample = jnp.stack(cs_l)
    return (xp, xs, k_prompt, v_prompt, conv_prompt, k_sample, v_sample, conv_sample)
```

```python
import functools
import math

import jax
import jax.numpy as jnp
from jax import lax
from jax.experimental import pallas as pl
from jax.experimental.pallas import tpu as pltpu

F32 = jnp.float32
BF16 = jnp.bfloat16

N_HEADS = 4
D_HEAD = 64
QK_DIM = 2 * D_HEAD
V_DIM = 128
CONV_K = 31
CONV_STATE = CONV_K - 1
N_ADA = 6
EPS = 1e-6
NEG = -1e30
Q_SCALE = math.log2(math.e) / math.sqrt(D_HEAD)

VMEM_LIMIT_BYTES = 56 * 1024 * 1024
HALO = 32


def _rms(x, g):
    return x * lax.rsqrt(jnp.mean(x * x, axis=-1, keepdims=True) + EPS) * g


def _silu(x):
    return x * jax.nn.sigmoid(x)


def _const_spec(shape):
    zeros = (0,) * len(shape)
    return pl.BlockSpec(shape, lambda *_: zeros, pipeline_mode=pl.Buffered(1))


def _lam_value(lq1, lk1, lq2, lk2, lam_init):
    a = jnp.sum(lq1 * lk1, axis=-1, keepdims=True)
    b = jnp.sum(lq2 * lk2, axis=-1, keepdims=True)
    return jnp.exp(a) - jnp.exp(b) + lam_init


def _ada_kernel(c_ref, w_ref, b_ref, o_ref):
    c = c_ref[...]
    a = _silu(c).astype(BF16)
    o_ref[...] = jnp.dot(a, w_ref[...].astype(BF16), preferred_element_type=F32) + b_ref[...]


def _ada(c_all, w_ada, b_ada):
    rows, d = c_all.shape
    n = w_ada.shape[1]
    tn = 1024 if n % 1024 == 0 else n
    return pl.pallas_call(
        _ada_kernel,
        out_shape=jax.ShapeDtypeStruct((rows, n), F32),
        grid=(n // tn,),
        in_specs=[
            pl.BlockSpec((rows, d), lambda j: (0, 0)),
            pl.BlockSpec((d, tn), lambda j: (0, j)),
            pl.BlockSpec((1, tn), lambda j: (0, j)),
        ],
        out_specs=pl.BlockSpec((rows, tn), lambda j: (0, j)),
        compiler_params=pltpu.CompilerParams(dimension_semantics=("arbitrary",)),
        name="ada_mod",
    )(c_all, w_ada, b_ada.reshape(1, n))


def _inproj_prompt_kernel(x_ref, mods_ref, g_ref, wkvg_ref, wqvt_ref, bglu_ref,
                          k_ref, v_ref, kbf_ref, qt_ref, vt_ref, u_ref, *, d, aw, cw):
    x = x_ref[0]
    shift = mods_ref[0, :, 0:d]
    scale = mods_ref[0, :, d:2 * d]
    h = _rms(x, g_ref[...]) * (1.0 + scale) + shift
    hb = h.astype(BF16)
    z = jnp.dot(hb, wkvg_ref[...], preferred_element_type=F32)
    k = z[:, 0:aw]
    k_ref[0] = k
    kbf_ref[0] = k.astype(BF16)
    v_ref[0] = z[:, aw:2 * aw]
    ga = z[:, 2 * aw:] + bglu_ref[...]
    u_ref[0] = ga[:, 0:cw] * jax.nn.sigmoid(ga[:, cw:])
    zt = lax.dot_general(wqvt_ref[...], hb, (((1,), (1,)), ((), ())),
                         preferred_element_type=F32)
    qt_ref[0] = (zt[0:aw] * Q_SCALE).astype(BF16)
    vt_ref[0] = zt[aw:].astype(BF16)


def _inproj_prompt(x, mods, g_pre, w_kvg, w_qvt, b_glu, *, tm):
    b, s, d = x.shape
    aw = w_qvt.shape[0] // 2
    cw = b_glu.shape[-1] // 2
    kern = functools.partial(_inproj_prompt_kernel, d=d, aw=aw, cw=cw)
    row_blk = lambda w: pl.BlockSpec((1, tm, w), lambda i, j: (i, j, 0))
    col_blk = pl.BlockSpec((1, aw, tm), lambda i, j: (i, 0, j))
    return pl.pallas_call(
        kern,
        out_shape=(
            jax.ShapeDtypeStruct((b, s, aw), F32),
            jax.ShapeDtypeStruct((b, s, aw), F32),
            jax.ShapeDtypeStruct((b, s, aw), BF16),
            jax.ShapeDtypeStruct((b, aw, s), BF16),
            jax.ShapeDtypeStruct((b, aw, s), BF16),
            jax.ShapeDtypeStruct((b, s, cw), F32),
        ),
        grid=(b, s // tm),
        in_specs=[
            row_blk(d),
            pl.BlockSpec((1, 1, N_ADA * d), lambda i, j: (i, 0, 0)),
            _const_spec((1, d)),
            _const_spec(w_kvg.shape),
            _const_spec(w_qvt.shape),
            _const_spec((1, 2 * cw)),
        ],
        out_specs=(row_blk(aw), row_blk(aw), row_blk(aw), col_blk, col_blk, row_blk(cw)),
        compiler_params=pltpu.CompilerParams(
            dimension_semantics=("arbitrary", "arbitrary"),
            vmem_limit_bytes=VMEM_LIMIT_BYTES),
        name="inproj_prompt",
    )(x, mods, g_pre, w_kvg, w_qvt, b_glu)


def _conv_prompt_kernel(u_ref, w_ref, cb_ref, lg_ref, lb_ref, yc_ref, ext_ref, *, tc, rc):
    si = pl.program_id(1)

    @pl.when(si == 0)
    def _():
        ext_ref[0:HALO, :] = jnp.zeros((HALO, ext_ref.shape[1]), F32)

    @pl.when(si > 0)
    def _():
        ext_ref[0:HALO, :] = ext_ref[tc:tc + HALO, :]

    ext_ref[HALO:HALO + tc, :] = u_ref[0]
    off = HALO - CONV_STATE
    for c in range(tc // rc):
        acc = jnp.zeros((rc, ext_ref.shape[1]), F32)
        for j in range(CONV_K):
            r0 = c * rc + off + j
            acc = acc + w_ref[j:j + 1, :] * ext_ref[r0:r0 + rc, :]
        y = acc + cb_ref[...]
        mu = jnp.mean(y, axis=-1, keepdims=True)
        yc = y - mu
        yn = yc * lax.rsqrt(jnp.mean(yc * yc, axis=-1, keepdims=True) + EPS)
        yn = yn * lg_ref[...] + lb_ref[...]
        yc_ref[0, c * rc:(c + 1) * rc, :] = _silu(yn).astype(yc_ref.dtype)


def _conv_prompt(u, conv_w, conv_b, ln_g, ln_b, *, tc):
    b, s, cw = u.shape
    rc = min(64, tc)
    kern = functools.partial(_conv_prompt_kernel, tc=tc, rc=rc)
    return pl.pallas_call(
        kern,
        out_shape=jax.ShapeDtypeStruct((b, s, cw), BF16),
        grid=(b, s // tc),
        in_specs=[
            pl.BlockSpec((1, tc, cw), lambda i, j: (i, j, 0)),
            _const_spec((CONV_K, cw)),
            _const_spec((1, cw)),
            _const_spec((1, cw)),
            _const_spec((1, cw)),
        ],
        out_specs=pl.BlockSpec((1, tc, cw), lambda i, j: (i, j, 0)),
        scratch_shapes=[pltpu.VMEM((HALO + tc, cw), F32)],
        compiler_params=pltpu.CompilerParams(
            dimension_semantics=("arbitrary", "arbitrary")),
        name="conv_prompt",
    )(u, conv_w, conv_b, ln_g, ln_b)


def _attn_prompt_kernel(qt_ref, k_ref, vt_ref, lq1_ref, lk1_ref, lq2_ref, lk2_ref, g_ref,
                        o_ref, acc1_ref, acc2_ref, *, tq, tk, lam_init):
    qi = pl.program_id(2)
    qt = qt_ref[0]
    row = lax.broadcasted_iota(jnp.int32, qt.shape, 0)
    zero = jnp.zeros_like(qt)
    w1 = jnp.where(row < D_HEAD, qt, zero)
    w2 = jnp.where(row >= D_HEAD, qt, zero)
    acc1_ref[...] = jnp.zeros_like(acc1_ref)
    acc2_ref[...] = jnp.zeros_like(acc2_ref)

    def update(s, m, l, acc_ref, vblk):
        m_new = jnp.maximum(m, jnp.max(s, axis=0, keepdims=True))
        alpha = jnp.exp2(m - m_new)
        p = jnp.exp2(s - m_new)
        l_new = alpha * l + jnp.sum(p, axis=0, keepdims=True)
        pv = jnp.dot(vblk, p.astype(BF16), preferred_element_type=F32)
        acc_ref[...] = alpha * acc_ref[...] + pv
        return m_new, l_new

    def chunk(start, carry, mask):
        m1, l1, m2, l2 = carry
        kblk = k_ref[0, pl.ds(start, tk), :]
        vblk = vt_ref[0, :, pl.ds(start, tk)]
        s1 = jnp.dot(kblk, w1, preferred_element_type=F32)
        s2 = jnp.dot(kblk, w2, preferred_element_type=F32)
        if mask is not None:
            s1 = jnp.where(mask, s1, NEG)
            s2 = jnp.where(mask, s2, NEG)
        m1, l1 = update(s1, m1, l1, acc1_ref, vblk)
        m2, l2 = update(s2, m2, l2, acc2_ref, vblk)
        return m1, l1, m2, l2

    init_m = jnp.full((1, tq), NEG, F32)
    init_l = jnp.zeros((1, tq), F32)
    per_q = tq // tk

    def body(ki, carry):
        return chunk(pl.multiple_of(ki * tk, tk), carry, None)

    carry = lax.fori_loop(0, qi * per_q, body, (init_m, init_l, init_m, init_l))
    key_rel = lax.broadcasted_iota(jnp.int32, (tk, tq), 0)
    q_rel = lax.broadcasted_iota(jnp.int32, (tk, tq), 1)
    for c in range(per_q):
        mask = (key_rel + c * tk) <= q_rel
        carry = chunk(pl.multiple_of(qi * tq + c * tk, tk), carry, mask)
    m1, l1, m2, l2 = carry

    lam = _lam_value(lq1_ref[...], lk1_ref[...], lq2_ref[...], lk2_ref[...], lam_init)
    ot = acc1_ref[...] * (1.0 / l1) - lam * (acc2_ref[...] * (1.0 / l2))
    ot = ot * lax.rsqrt(jnp.mean(ot * ot, axis=0, keepdims=True) + EPS)
    o = ot.T * g_ref[...] * (1.0 - lam_init)
    o_ref[0] = o.astype(o_ref.dtype)


def _attn_prompt(qt, kbf, vt, lam_rows, subln_g, *, tq, tk, lam_init):
    b, aw, s = qt.shape
    kern = functools.partial(_attn_prompt_kernel, tq=tq, tk=tk, lam_init=lam_init)
    small = _const_spec((1, D_HEAD))
    return pl.pallas_call(
        kern,
        out_shape=jax.ShapeDtypeStruct((b, s, aw), BF16),
        grid=(b, N_HEADS, s // tq),
        in_specs=[
            pl.BlockSpec((1, QK_DIM, tq), lambda i, h, j: (i, h, j)),
            pl.BlockSpec((1, s, QK_DIM), lambda i, h, j: (i, 0, h)),
            pl.BlockSpec((1, V_DIM, s), lambda i, h, j: (i, h, 0)),
            small, small, small, small,
            _const_spec((1, V_DIM)),
        ],
        out_specs=pl.BlockSpec((1, tq, V_DIM), lambda i, h, j: (i, j, h)),
        scratch_shapes=[pltpu.VMEM((V_DIM, tq), F32), pltpu.VMEM((V_DIM, tq), F32)],
        compiler_params=pltpu.CompilerParams(
            dimension_semantics=("arbitrary", "arbitrary", "arbitrary"),
            vmem_limit_bytes=VMEM_LIMIT_BYTES),
        name="attn_prompt",
    )(qt, kbf, vt, *lam_rows, subln_g)


def _post_kernel(ya_ref, yc_ref, x_ref, mods_ref, gpm_ref, gpf_ref, gpo_ref,
                 wout_ref, wg_ref, wu_ref, wd_ref, y_ref, acc_ref, *, d, aw, ff_chunks):
    gate1 = mods_ref[0, :, 2 * d:3 * d]
    shift2 = mods_ref[0, :, 3 * d:4 * d]
    scale2 = mods_ref[0, :, 4 * d:5 * d]
    gate2 = mods_ref[0, :, 5 * d:6 * d]
    m = jnp.dot(ya_ref[0], wout_ref[0:aw, :], preferred_element_type=F32)
    m = m + jnp.dot(yc_ref[0], wout_ref[aw:, :], preferred_element_type=F32)
    x1 = x_ref[0] + gate1 * _rms(m, gpm_ref[...])
    hb = (_rms(x1, gpf_ref[...]) * (1.0 + scale2) + shift2).astype(BF16)
    for n, (c0, c1) in enumerate(ff_chunks):
        g = jnp.dot(hb, wg_ref[:, c0:c1], preferred_element_type=F32)
        u = jnp.dot(hb, wu_ref[:, c0:c1], preferred_element_type=F32)
        a = (_silu(g) * u).astype(BF16)
        part = jnp.dot(a, wd_ref[c0:c1, :], preferred_element_type=F32)
        if n == 0:
            acc_ref[...] = part
        else:
            acc_ref[...] += part
    y_ref[0] = x1 + gate2 * _rms(acc_ref[...], gpo_ref[...])


def _post(ya, yc, x, mods, g_post_mix, g_pre_ffn, g_post_ffn, w_out, w_gate, w_up, w_down, *, tm):
    b, s, d = x.shape
    aw = ya.shape[-1]
    cw = yc.shape[-1]
    dff = w_gate.shape[1]
    step = 512
    ff_chunks = tuple((c, min(c + step, dff)) for c in range(0, dff, step))
    mod_rows = mods.shape[1]
    mt = 1 if mod_rows == 1 else tm
    kern = functools.partial(_post_kernel, d=d, aw=aw, ff_chunks=ff_chunks)
    return pl.pallas_call(
        kern,
        out_shape=jax.ShapeDtypeStruct((b, s, d), F32),
        grid=(b, s // tm),
        in_specs=[
            pl.BlockSpec((1, tm, aw), lambda i, j: (i, j, 0)),
            pl.BlockSpec((1, tm, cw), lambda i, j: (i, j, 0)),
            pl.BlockSpec((1, tm, d), lambda i, j: (i, j, 0)),
            pl.BlockSpec((1, mt, N_ADA * d),
                         (lambda i, j: (i, 0, 0)) if mod_rows == 1 else (lambda i, j: (i, j, 0))),
            _const_spec((1, d)), _const_spec((1, d)), _const_spec((1, d)),
            _const_spec(w_out.shape), _const_spec(w_gate.shape),
            _const_spec(w_up.shape), _const_spec(w_down.shape),
        ],
        out_specs=pl.BlockSpec((1, tm, d), lambda i, j: (i, j, 0)),
        scratch_shapes=[pltpu.VMEM((tm, d), F32)],
        compiler_params=pltpu.CompilerParams(
            dimension_semantics=("arbitrary", "arbitrary"),
            vmem_limit_bytes=VMEM_LIMIT_BYTES),
        name="post_ffn",
    )(ya, yc, x, mods, g_post_mix, g_pre_ffn, g_post_ffn, w_out, w_gate, w_up, w_down)


def _front_sample_kernel(x_ref, mods_ref, g_ref, win_ref, bglu_ref, st_ref, cw_ref, cb_ref,
                         lg_ref, lb_ref, q_ref, k_ref, v_ref, u_ref, yc_ref, *, d, aw, cw):
    x = x_ref[...]
    shift = mods_ref[:, 0:d]
    scale = mods_ref[:, d:2 * d]
    hb = (_rms(x, g_ref[...]) * (1.0 + scale) + shift).astype(BF16)
    z = jnp.dot(hb, win_ref[...], preferred_element_type=F32)
    q_ref[...] = z[:, 0:aw] * Q_SCALE
    k_ref[...] = z[:, aw:2 * aw]
    v_ref[...] = z[:, 2 * aw:3 * aw]
    ga = z[:, 3 * aw:] + bglu_ref[...]
    u = ga[:, 0:cw] * jax.nn.sigmoid(ga[:, cw:])
    u_ref[...] = u
    y = cw_ref[CONV_STATE:CONV_K, :] * u + cb_ref[...]
    for j in range(CONV_STATE):
        y = y + cw_ref[j:j + 1, :] * st_ref[j]
    mu = jnp.mean(y, axis=-1, keepdims=True)
    yc = y - mu
    yn = yc * lax.rsqrt(jnp.mean(yc * yc, axis=-1, keepdims=True) + EPS)
    yc_ref[...] = _silu(yn * lg_ref[...] + lb_ref[...]).astype(yc_ref.dtype)


def _front_sample(x, mods, g_pre, w_in, b_glu, state_t, conv_w, conv_b, ln_g, ln_b):
    rows, d = x.shape
    cw = conv_w.shape[1]
    aw = (w_in.shape[1] - 2 * cw) // 3
    kern = functools.partial(_front_sample_kernel, d=d, aw=aw, cw=cw)
    full = lambda a: pl.BlockSpec(a.shape, lambda i: (0,) * a.ndim)
    args = (x, mods, g_pre, w_in, b_glu, state_t, conv_w, conv_b, ln_g, ln_b)
    out_shapes = (
        jax.ShapeDtypeStruct((rows, aw), F32),
        jax.ShapeDtypeStruct((rows, aw), F32),
        jax.ShapeDtypeStruct((rows, aw), F32),
        jax.ShapeDtypeStruct((rows, cw), F32),
        jax.ShapeDtypeStruct((rows, cw), BF16),
    )
    return pl.pallas_call(
        kern,
        out_shape=out_shapes,
        grid=(1,),
        in_specs=[full(a) for a in args],
        out_specs=tuple(pl.BlockSpec(o.shape, lambda i: (0, 0)) for o in out_shapes),
        compiler_params=pltpu.CompilerParams(
            dimension_semantics=("arbitrary",), vmem_limit_bytes=VMEM_LIMIT_BYTES),
        name="front_sample",
    )(*args)


ROWS = 16


def _attn_sample_kernel(pt_ref, q_ref, kn_ref, vn_ref, lq1_ref, lk1_ref, lq2_ref, lk2_ref, g_ref,
                        *rest, pp, lam_init):
    k_refs = rest[0:pp]
    v_refs = rest[pp:2 * pp]
    o_ref, m_ref, l_ref, acc_ref = rest[2 * pp:]
    del pt_ref
    j = pl.program_id(1)
    aw = q_ref.shape[-1]
    sub = lax.broadcasted_iota(jnp.int32, (ROWS, aw), 0)
    lane = lax.broadcasted_iota(jnp.int32, (ROWS, aw), 1)
    sel = lax.shift_right_logical(lane, int(math.log2(D_HEAD))) == sub
    qbd = jnp.where(sel, q_ref[0], 0.0)
    qbd_bf = qbd.astype(BF16)

    @pl.when(j == 0)
    def _():
        m_ref[...] = jnp.full(m_ref.shape, NEG, F32)
        l_ref[...] = jnp.zeros(l_ref.shape, F32)
        acc_ref[...] = jnp.zeros(acc_ref.shape, F32)

    s = jnp.concatenate(
        [lax.dot_general(qbd_bf, k_refs[i][0].astype(BF16), (((1,), (1,)), ((), ())),
                         preferred_element_type=F32) for i in range(pp)], axis=1)
    m_old = m_ref[...]
    m_new = jnp.maximum(m_old, jnp.max(s, axis=1, keepdims=True))
    alpha = jnp.exp2(m_old - m_new)
    p = jnp.exp2(s - m_new)
    l_ref[...] = alpha * l_ref[...] + jnp.sum(p, axis=1, keepdims=True)
    m_ref[...] = m_new
    page = s.shape[1] // pp
    pv = jnp.zeros(acc_ref.shape, F32)
    for i in range(pp):
        pv = pv + jnp.dot(p[:, i * page:(i + 1) * page].astype(BF16), v_refs[i][0].astype(BF16),
                          preferred_element_type=F32)
    acc_ref[...] = alpha * acc_ref[...] + pv

    @pl.when(j == pl.num_programs(1) - 1)
    def _():
        s_new = jnp.sum(qbd * kn_ref[0], axis=1, keepdims=True)
        m_old = m_ref[...]
        m_fin = jnp.maximum(m_old, s_new)
        alpha = jnp.exp2(m_old - m_fin)
        p_new = jnp.exp2(s_new - m_fin)
        l_fin = alpha * l_ref[...] + p_new
        acc = alpha * acc_ref[...] + p_new * vn_ref[0]
        o = acc * (1.0 / l_fin)
        head_of_lane = lax.shift_right_logical(lane, int(math.log2(V_DIM)))
        head_of_row = lax.shift_right_logical(sub, 1)
        map_of_row = jnp.bitwise_and(sub, 1)
        own = (head_of_lane == head_of_row) & (sub < 2 * N_HEADS)
        o1 = jnp.sum(jnp.where(own & (map_of_row == 0), o, 0.0), axis=0, keepdims=True)
        o2 = jnp.sum(jnp.where(own & (map_of_row == 1), o, 0.0), axis=0, keepdims=True)
        lam = _lam_value(lq1_ref[...], lk1_ref[...], lq2_ref[...], lk2_ref[...], lam_init)
        oc = o1 - lam * o2
        parts = []
        for h in range(N_HEADS):
            seg = oc[:, h * V_DIM:(h + 1) * V_DIM]
            seg = seg * lax.rsqrt(jnp.mean(seg * seg, axis=-1, keepdims=True) + EPS)
            parts.append(seg * g_ref[...] * (1.0 - lam_init))
        o_ref[0] = jnp.concatenate(parts, axis=1).astype(o_ref.dtype)


def _attn_sample(page_table, q, k_new, v_new, lam_rows, subln_g, cache_k, cache_v, *, pp, lam_init):
    db, n_pages = page_table.shape
    n_pool, page, aw = cache_k.shape
    kern = functools.partial(_attn_sample_kernel, pp=pp, lam_init=lam_init)
    row = pl.BlockSpec((1, 1, aw), lambda b, j, pt: (b, 0, 0))
    small = pl.BlockSpec((1, D_HEAD), lambda b, j, pt: (0, 0))

    def page_spec(i):
        return pl.BlockSpec((1, page, aw), lambda b, j, pt: (pt[b, j * pp + i], 0, 0))

    grid_spec = pltpu.PrefetchScalarGridSpec(
        num_scalar_prefetch=1,
        grid=(db, n_pages // pp),
        in_specs=[row, row, row, small, small, small, small,
                  pl.BlockSpec((1, V_DIM), lambda b, j, pt: (0, 0))]
                 + [page_spec(i) for i in range(pp)] + [page_spec(i) for i in range(pp)],
        out_specs=pl.BlockSpec((1, 1, aw), lambda b, j, pt: (b, 0, 0)),
        scratch_shapes=[pltpu.VMEM((ROWS, 1), F32), pltpu.VMEM((ROWS, 1), F32),
                        pltpu.VMEM((ROWS, aw), F32)],
    )
    return pl.pallas_call(
        kern,
        out_shape=jax.ShapeDtypeStruct((db, 1, aw), BF16),
        grid_spec=grid_spec,
        compiler_params=pltpu.CompilerParams(
            dimension_semantics=("arbitrary", "arbitrary"),
            vmem_limit_bytes=VMEM_LIMIT_BYTES),
        name="attn_sample",
    )(page_table, q.reshape(db, 1, aw), k_new.reshape(db, 1, aw), v_new.reshape(db, 1, aw),
      *lam_rows, subln_g, *([cache_k] * pp), *([cache_v] * pp))


def _tile(n, pref):
    t = min(n, pref)
    while n % t:
        t //= 2
    return t


def kernel(x_prompt, x_sample, c_prompt, c_sample, cache_k, cache_v, state_conv, page_table, w_ada, b_ada, g_pre_mix, g_post_mix, g_pre_ffn, g_post_ffn, w_in, b_glu, conv_w, conv_b, conv_ln_g, conv_ln_b, lambda_q1, lambda_k1, lambda_q2, lambda_k2, subln_g, w_out, w_gate, w_up, w_down):
    depth = w_in.shape[0]
    assert depth == 1, "single-layer step"
    l = 0
    lam_init = 0.8 - 0.6 * math.exp(-0.3 * l)
    b, s, d = x_prompt.shape
    db = x_sample.shape[0]
    assert x_sample.shape[1] == 1
    cw = conv_w.shape[-1]
    aw = N_HEADS * V_DIM
    n_pool, page = cache_k.shape[1], cache_k.shape[2]

    row = lambda a: a[l].reshape(1, -1)
    lam_rows = (row(lambda_q1), row(lambda_k1), row(lambda_q2), row(lambda_k2))
    g_sub = row(subln_g)

    mods = _ada(jnp.concatenate([c_prompt, c_sample], axis=0), w_ada[l], b_ada[l])
    mods_p = mods[:b].reshape(b, 1, N_ADA * d)
    mods_s = mods[b:]

    w_in_bf = w_in[l].astype(BF16)
    w_kvg = w_in_bf[:, aw:]
    w_qvt = jnp.concatenate([w_in_bf[:, 0:aw], w_in_bf[:, 2 * aw:3 * aw]], axis=1).T
    w_out_bf = w_out[l].astype(BF16)
    w_gate_bf = w_gate[l].astype(BF16)
    w_up_bf = w_up[l].astype(BF16)
    w_down_bf = w_down[l].astype(BF16)

    tm = _tile(s, 512)
    k_p, v_p, kbf, qt, vt, u_p = _inproj_prompt(
        x_prompt, mods_p, row(g_pre_mix), w_kvg, w_qvt, row(b_glu), tm=tm)
    yc_p = _conv_prompt(u_p, conv_w[l], row(conv_b), row(conv_ln_g), row(conv_ln_b), tc=tm)
    tq = _tile(s, 512)
    ya_p = _attn_prompt(qt, kbf, vt, lam_rows, g_sub, tq=tq, tk=tq, lam_init=lam_init)
    y_prompt = _post(ya_p, yc_p, x_prompt, mods_p, row(g_post_mix), row(g_pre_ffn), row(g_post_ffn),
                     w_out_bf, w_gate_bf, w_up_bf, w_down_bf, tm=tm)
    if s >= CONV_STATE:
        conv_p = u_p[:, s - CONV_STATE:, :]
    else:
        conv_p = jnp.pad(u_p, ((0, 0), (CONV_STATE - s, 0), (0, 0)))

    state = state_conv[l]
    q_s, k_s, v_s, u_s, yc_s = _front_sample(
        x_sample.reshape(db, d), mods_s, row(g_pre_mix), w_in_bf, row(b_glu),
        jnp.transpose(state, (1, 0, 2)), conv_w[l], row(conv_b), row(conv_ln_g), row(conv_ln_b))
    pp = _tile(page_table.shape[1], 8)
    ya_s = _attn_sample(page_table, q_s, k_s, v_s, lam_rows, g_sub,
                        cache_k[l].reshape(n_pool, page, aw), cache_v[l].reshape(n_pool, page, aw),
                        pp=pp, lam_init=lam_init)
    y_sample = _post(ya_s.reshape(1, db, aw), yc_s.reshape(1, db, cw), x_sample.reshape(1, db, d),
                     mods_s.reshape(1, db, N_ADA * d), row(g_post_mix), row(g_pre_ffn),
                     row(g_post_ffn), w_out_bf, w_gate_bf, w_up_bf, w_down_bf, tm=_tile(db, 512))
    conv_s = jnp.concatenate([state[:, 1:, :], u_s[:, None, :]], axis=1)

    return (y_prompt,
            y_sample.reshape(db, 1, d),
            k_p.reshape(1, b, s, N_HEADS, QK_DIM),
            v_p.reshape(1, b, s, N_HEADS, V_DIM),
            conv_p[None],
            k_s.reshape(1, db, 1, N_HEADS, QK_DIM),
            v_s.reshape(1, db, 1, N_HEADS, V_DIM),
            conv_s[None])
```

```python
import functools
import math

import jax
import jax.numpy as jnp
from jax import lax
from jax.experimental import pallas as pl
from jax.experimental.pallas import tpu as pltpu

F32 = jnp.float32
BF16 = jnp.bfloat16

N_HEADS = 4
D_HEAD = 64
QK_DIM = 2 * D_HEAD
V_DIM = 128
CONV_K = 31
CONV_STATE = CONV_K - 1
N_ADA = 6
EPS = 1e-6
NEG = -1e30
Q_SCALE = math.log2(math.e) / math.sqrt(D_HEAD)

VMEM_LIMIT_BYTES = 56 * 1024 * 1024
SUBLANES = 8
LANES = 128
BF16_ROWS = 16
ACC_ROWS = V_DIM + SUBLANES
HALO = 32


def _rms(x, g):
    return x * lax.rsqrt(jnp.mean(x * x, axis=-1, keepdims=True) + EPS) * g


def _silu(x):
    return x * jax.nn.sigmoid(x)


def _const_spec(shape):
    zeros = (0,) * len(shape)
    return pl.BlockSpec(shape, lambda *_: zeros, pipeline_mode=pl.Buffered(1))


def _lam_value(lq1, lk1, lq2, lk2, lam_init):
    a = jnp.sum(lq1 * lk1, axis=-1, keepdims=True)
    b = jnp.sum(lq2 * lk2, axis=-1, keepdims=True)
    return jnp.exp(a) - jnp.exp(b) + lam_init


def _ada_kernel(c_ref, w_ref, b_ref, o_ref):
    c = c_ref[...]
    a = _silu(c).astype(BF16)
    o_ref[...] = jnp.dot(a, w_ref[...].astype(BF16), preferred_element_type=F32) + b_ref[...]


def _ada(c_all, w_ada, b_ada):
    rows, d = c_all.shape
    n = w_ada.shape[1]
    tn = 1024 if n % 1024 == 0 else n
    return pl.pallas_call(
        _ada_kernel,
        out_shape=jax.ShapeDtypeStruct((rows, n), F32),
        grid=(n // tn,),
        in_specs=[
            pl.BlockSpec((rows, d), lambda j: (0, 0)),
            pl.BlockSpec((d, tn), lambda j: (0, j)),
            pl.BlockSpec((1, tn), lambda j: (0, j)),
        ],
        out_specs=pl.BlockSpec((rows, tn), lambda j: (0, j)),
        compiler_params=pltpu.CompilerParams(dimension_semantics=("arbitrary",)),
        name="ada_mod",
    )(c_all, w_ada, b_ada.reshape(1, n))


def _conv_ln_swish(ext_ref, sh_ref, y_ref, w_ref, cb_ref, lg_ref, lb_ref, yc_ref, *, tm, rc):
    n_sh = sh_ref.shape[1]
    for r in range(1, SUBLANES):
        sh_ref[r - 1] = ext_ref[r:r + n_sh, :]
    off = HALO - CONV_STATE
    for q in range(ext_ref.shape[1] // LANES):
        lanes = slice(q * LANES, (q + 1) * LANES)
        taps = [w_ref[j:j + 1, lanes] for j in range(CONV_K)]
        for c in range(tm // rc):
            acc = jnp.zeros((rc, LANES), F32)
            for j in range(CONV_K):
                a, r = divmod(off + j, SUBLANES)
                r0 = c * rc + a * SUBLANES
                win = ext_ref[r0:r0 + rc, lanes] if r == 0 else sh_ref[r - 1, r0:r0 + rc, lanes]
                acc = acc + taps[j] * win
            y_ref[c * rc:(c + 1) * rc, lanes] = acc
    for c in range(tm // rc):
        y = y_ref[c * rc:(c + 1) * rc, :] + cb_ref[...]
        mu = jnp.mean(y, axis=-1, keepdims=True)
        yc = y - mu
        yn = yc * lax.rsqrt(jnp.mean(yc * yc, axis=-1, keepdims=True) + EPS)
        yn = yn * lg_ref[...] + lb_ref[...]
        yc_ref[0, c * rc:(c + 1) * rc, :] = _silu(yn).astype(yc_ref.dtype)


def _front_prompt_kernel(x_ref, mods_ref, g_ref, wglu_ref, wkv_ref, wqvt_ref, bglu_ref,
                         cw_ref, cb_ref, lg_ref, lb_ref,
                         k_ref, v_ref, kbf_ref, qt_ref, vt_ref, yc_ref, tail_ref,
                         ext_ref, sh_ref, y_ref, *, d, aw, cw, tm, rc):
    si = pl.program_id(1)
    x = x_ref[0]
    shift = mods_ref[0, :, 0:d]
    scale = mods_ref[0, :, d:2 * d]
    h = _rms(x, g_ref[...]) * (1.0 + scale) + shift
    hb = h.astype(BF16)

    ga = jnp.dot(hb, wglu_ref[...], preferred_element_type=F32) + bglu_ref[...]
    u = ga[:, 0:cw] * jax.nn.sigmoid(ga[:, cw:])

    @pl.when(si == 0)
    def _():
        ext_ref[0:HALO, :] = jnp.zeros((HALO, cw), F32)

    @pl.when(si > 0)
    def _():
        ext_ref[0:HALO, :] = ext_ref[tm:tm + HALO, :]

    ext_ref[HALO:HALO + tm, :] = u
    tail_ref[0] = u[tm - HALO:tm, :]
    _conv_ln_swish(ext_ref, sh_ref, y_ref, cw_ref, cb_ref, lg_ref, lb_ref, yc_ref, tm=tm, rc=rc)

    z = jnp.dot(hb, wkv_ref[...], preferred_element_type=F32)
    k = z[:, 0:aw]
    v = z[:, aw:2 * aw]
    for hd in range(N_HEADS):
        k_ref[0, pl.ds(hd, tm, stride=N_HEADS), :] = k[:, hd * QK_DIM:(hd + 1) * QK_DIM]
        v_ref[0, pl.ds(hd, tm, stride=N_HEADS), :] = v[:, hd * V_DIM:(hd + 1) * V_DIM]
    kbf_ref[0] = k.astype(BF16)
    zt = lax.dot_general(wqvt_ref[...], hb, (((1,), (1,)), ((), ())),
                         preferred_element_type=F32)
    qt_ref[0] = (zt[0:aw] * Q_SCALE).astype(BF16)
    vt_ref[0] = zt[aw:].astype(BF16)


def _front_prompt(x, mods, g_pre, w_glu, w_kv, w_qvt, b_glu, conv_w, conv_b, ln_g, ln_b, *, tm):
    b, s, d = x.shape
    aw = w_qvt.shape[0] // 2
    cw = b_glu.shape[-1] // 2
    assert tm >= HALO and tm % SUBLANES == 0
    rc = min(128, tm)
    kern = functools.partial(_front_prompt_kernel, d=d, aw=aw, cw=cw, tm=tm, rc=rc)
    row_blk = lambda w: pl.BlockSpec((1, tm, w), lambda i, j: (i, j, 0))
    head_blk = lambda w: pl.BlockSpec((1, tm * N_HEADS, w), lambda i, j: (i, j, 0))
    col_blk = pl.BlockSpec((1, aw, tm), lambda i, j: (i, 0, j))
    return pl.pallas_call(
        kern,
        out_shape=(
            jax.ShapeDtypeStruct((b, s * N_HEADS, QK_DIM), F32),
            jax.ShapeDtypeStruct((b, s * N_HEADS, V_DIM), F32),
            jax.ShapeDtypeStruct((b, s, aw), BF16),
            jax.ShapeDtypeStruct((b, aw, s), BF16),
            jax.ShapeDtypeStruct((b, aw, s), BF16),
            jax.ShapeDtypeStruct((b, s, cw), BF16),
            jax.ShapeDtypeStruct((b, HALO, cw), F32),
        ),
        grid=(b, s // tm),
        in_specs=[
            row_blk(d),
            pl.BlockSpec((1, 1, N_ADA * d), lambda i, j: (i, 0, 0)),
            _const_spec((1, d)),
            _const_spec(w_glu.shape),
            _const_spec(w_kv.shape),
            _const_spec(w_qvt.shape),
            _const_spec((1, 2 * cw)),
            _const_spec((CONV_K, cw)),
            _const_spec((1, cw)),
            _const_spec((1, cw)),
            _const_spec((1, cw)),
        ],
        out_specs=(head_blk(QK_DIM), head_blk(V_DIM), row_blk(aw), col_blk, col_blk, row_blk(cw),
                   pl.BlockSpec((1, HALO, cw), lambda i, j: (i, 0, 0))),
        scratch_shapes=[pltpu.VMEM((HALO + tm, cw), F32),
                        pltpu.VMEM((SUBLANES - 1, HALO + tm - SUBLANES, cw), F32),
                        pltpu.VMEM((tm, cw), F32)],
        compiler_params=pltpu.CompilerParams(
            dimension_semantics=("arbitrary", "arbitrary"),
            vmem_limit_bytes=VMEM_LIMIT_BYTES),
        name="front_prompt",
    )(x, mods, g_pre, w_glu, w_kv, w_qvt, b_glu, conv_w, conv_b, ln_g, ln_b)


def _attn_prompt_kernel(qt_ref, k_ref, vt_ref, lq1_ref, lk1_ref, lq2_ref, lk2_ref, g_ref,
                        o_ref, w_ref, sa_ref, sb_ref, acc_ref, *, tq, lam_init):
    tk = tq // 2
    qi = pl.program_id(2)
    qt = qt_ref[0]
    row = lax.broadcasted_iota(jnp.int32, qt.shape, 0)
    zero = jnp.zeros_like(qt)
    w_ref[0] = jnp.where(row < D_HEAD, qt, zero)
    w_ref[1] = jnp.where(row >= D_HEAD, qt, zero)
    acc_ref[...] = jnp.zeros_like(acc_ref)
    ones_rows = (lax.broadcasted_iota(jnp.int32, (BF16_ROWS, tk), 0) == 0).astype(BF16)

    def key_slice(chunk):
        start = chunk * tk
        return pl.ds(start if isinstance(start, int) else pl.multiple_of(start, tk), tk)

    def scores(chunk, s_ref, lo=0):
        kblk = k_ref[0, key_slice(chunk), :]
        for c in range(2):
            s_ref[c, :, lo:] = jnp.dot(kblk, w_ref[c, :, lo:], preferred_element_type=F32)

    def softmax_pv(chunk, s_ref, carry, mask, lo=0):
        vblk = jnp.concatenate([vt_ref[0, :, key_slice(chunk)], ones_rows], axis=0)
        out = []
        for c in range(2):
            m = carry[c][:, lo:]
            s = s_ref[c, :, lo:]
            if mask is not None:
                s = jnp.where(mask[:, lo:], s, NEG)
            m_new = jnp.maximum(m, jnp.max(s, axis=0, keepdims=True))
            alpha = jnp.exp2(m - m_new)
            p = jnp.exp2((s - m_new).astype(BF16))
            pv = jnp.dot(vblk, p, preferred_element_type=F32)
            acc_ref[c, :, lo:] = alpha * acc_ref[c, :, lo:] + pv[0:ACC_ROWS]
            out.append(m_new if lo == 0 else jnp.concatenate([carry[c][:, :lo], m_new], axis=1))
        return tuple(out)

    init = jnp.full((1, tq), NEG, F32)
    scores(0, sa_ref)

    def pair(j, carry):
        scores(2 * j + 1, sb_ref)
        carry = softmax_pv(2 * j, sa_ref, carry, None)
        scores(2 * j + 2, sa_ref)
        return softmax_pv(2 * j + 1, sb_ref, carry, None)

    carry = lax.fori_loop(0, qi, pair, (init, init))
    key_rel = lax.broadcasted_iota(jnp.int32, (tk, tq), 0)
    q_rel = lax.broadcasted_iota(jnp.int32, (tk, tq), 1)
    scores(2 * qi + 1, sb_ref, lo=tk)
    carry = softmax_pv(2 * qi, sa_ref, carry, key_rel <= q_rel)
    softmax_pv(2 * qi + 1, sb_ref, carry, key_rel + tk <= q_rel, lo=tk)

    lam = _lam_value(lq1_ref[...], lk1_ref[...], lq2_ref[...], lk2_ref[...], lam_init)
    l1 = acc_ref[0, V_DIM:V_DIM + 1, :]
    l2 = acc_ref[1, V_DIM:V_DIM + 1, :]
    ot = acc_ref[0, 0:V_DIM, :] * (1.0 / l1) - lam * (acc_ref[1, 0:V_DIM, :] * (1.0 / l2))
    ot = ot * lax.rsqrt(jnp.mean(ot * ot, axis=0, keepdims=True) + EPS)
    o = ot.T * g_ref[...] * (1.0 - lam_init)
    o_ref[0] = o.astype(o_ref.dtype)


def _attn_prompt(qt, kbf, vt, lam_rows, subln_g, *, tq, lam_init):
    b, aw, s = qt.shape
    tk = tq // 2
    kern = functools.partial(_attn_prompt_kernel, tq=tq, lam_init=lam_init)
    small = _const_spec((1, D_HEAD))
    return pl.pallas_call(
        kern,
        out_shape=jax.ShapeDtypeStruct((b, s, aw), BF16),
        grid=(b, N_HEADS, s // tq),
        in_specs=[
            pl.BlockSpec((1, QK_DIM, tq), lambda i, h, j: (i, h, j)),
            pl.BlockSpec((1, s, QK_DIM), lambda i, h, j: (i, 0, h)),
            pl.BlockSpec((1, V_DIM, s), lambda i, h, j: (i, h, 0)),
            small, small, small, small,
            _const_spec((1, V_DIM)),
        ],
        out_specs=pl.BlockSpec((1, tq, V_DIM), lambda i, h, j: (i, j, h)),
        scratch_shapes=[pltpu.VMEM((2, QK_DIM, tq), BF16),
                        pltpu.VMEM((2, tk, tq), F32), pltpu.VMEM((2, tk, tq), F32),
                        pltpu.VMEM((2, ACC_ROWS, tq), F32)],
        compiler_params=pltpu.CompilerParams(
            dimension_semantics=("arbitrary", "arbitrary", "arbitrary"),
            vmem_limit_bytes=VMEM_LIMIT_BYTES),
        name="attn_prompt",
    )(qt, kbf, vt, *lam_rows, subln_g)


def _post_kernel(ya_ref, yc_ref, x_ref, mods_ref, gpm_ref, gpf_ref, gpo_ref,
                 wout_ref, wg_ref, wu_ref, wd_ref, y_ref, acc_ref, *, d, aw, ff_chunks):
    gate1 = mods_ref[0, :, 2 * d:3 * d]
    shift2 = mods_ref[0, :, 3 * d:4 * d]
    scale2 = mods_ref[0, :, 4 * d:5 * d]
    gate2 = mods_ref[0, :, 5 * d:6 * d]
    m = jnp.dot(ya_ref[0].astype(BF16), wout_ref[0:aw, :], preferred_element_type=F32)
    m = m + jnp.dot(yc_ref[0].astype(BF16), wout_ref[aw:, :], preferred_element_type=F32)
    x1 = x_ref[0] + gate1 * _rms(m, gpm_ref[...])
    hb = (_rms(x1, gpf_ref[...]) * (1.0 + scale2) + shift2).astype(BF16)
    for n, (c0, c1) in enumerate(ff_chunks):
        g = jnp.dot(hb, wg_ref[:, c0:c1], preferred_element_type=F32)
        u = jnp.dot(hb, wu_ref[:, c0:c1], preferred_element_type=F32)
        a = (_silu(g) * u).astype(BF16)
        part = jnp.dot(a, wd_ref[c0:c1, :], preferred_element_type=F32)
        if n == 0:
            acc_ref[...] = part
        else:
            acc_ref[...] += part
    y_ref[0] = x1 + gate2 * _rms(acc_ref[...], gpo_ref[...])


def _post(ya, yc, x, mods, g_post_mix, g_pre_ffn, g_post_ffn, w_out, w_gate, w_up, w_down, *, tm):
    b, s, d = x.shape
    aw = ya.shape[-1]
    cw = yc.shape[-1]
    dff = w_gate.shape[1]
    step = 512
    ff_chunks = tuple((c, min(c + step, dff)) for c in range(0, dff, step))
    mod_rows = mods.shape[1]
    mt = 1 if mod_rows == 1 else tm
    kern = functools.partial(_post_kernel, d=d, aw=aw, ff_chunks=ff_chunks)
    return pl.pallas_call(
        kern,
        out_shape=jax.ShapeDtypeStruct((b, s, d), F32),
        grid=(b, s // tm),
        in_specs=[
            pl.BlockSpec((1, tm, aw), lambda i, j: (i, j, 0)),
            pl.BlockSpec((1, tm, cw), lambda i, j: (i, j, 0)),
            pl.BlockSpec((1, tm, d), lambda i, j: (i, j, 0)),
            pl.BlockSpec((1, mt, N_ADA * d),
                         (lambda i, j: (i, 0, 0)) if mod_rows == 1 else (lambda i, j: (i, j, 0))),
            _const_spec((1, d)), _const_spec((1, d)), _const_spec((1, d)),
            _const_spec(w_out.shape), _const_spec(w_gate.shape),
            _const_spec(w_up.shape), _const_spec(w_down.shape),
        ],
        out_specs=pl.BlockSpec((1, tm, d), lambda i, j: (i, j, 0)),
        scratch_shapes=[pltpu.VMEM((tm, d), F32)],
        compiler_params=pltpu.CompilerParams(
            dimension_semantics=("arbitrary", "arbitrary"),
            vmem_limit_bytes=VMEM_LIMIT_BYTES),
        name="post_ffn",
    )(ya, yc, x, mods, g_post_mix, g_pre_ffn, g_post_ffn, w_out, w_gate, w_up, w_down)


def _front_sample_kernel(x_ref, mods_ref, g_ref, win_ref, bglu_ref, st_ref, cw_ref, cb_ref,
                         lg_ref, lb_ref, q_ref, k_ref, v_ref, u_ref, yc_ref, *, d, aw, cw):
    x = x_ref[...]
    shift = mods_ref[:, 0:d]
    scale = mods_ref[:, d:2 * d]
    hb = (_rms(x, g_ref[...]) * (1.0 + scale) + shift).astype(BF16)
    z = jnp.dot(hb, win_ref[...], preferred_element_type=F32)
    q_ref[...] = z[:, 0:aw] * Q_SCALE
    k_ref[...] = z[:, aw:2 * aw]
    v_ref[...] = z[:, 2 * aw:3 * aw]
    ga = z[:, 3 * aw:] + bglu_ref[...]
    u = ga[:, 0:cw] * jax.nn.sigmoid(ga[:, cw:])
    u_ref[...] = u
    y = cw_ref[CONV_STATE:CONV_K, :] * u + cb_ref[...]
    for j in range(CONV_STATE):
        y = y + cw_ref[j:j + 1, :] * st_ref[j]
    mu = jnp.mean(y, axis=-1, keepdims=True)
    yc = y - mu
    yn = yc * lax.rsqrt(jnp.mean(yc * yc, axis=-1, keepdims=True) + EPS)
    yc_ref[...] = _silu(yn * lg_ref[...] + lb_ref[...]).astype(yc_ref.dtype)


def _front_sample(x, mods, g_pre, w_in, b_glu, state_t, conv_w, conv_b, ln_g, ln_b):
    rows, d = x.shape
    cw = conv_w.shape[1]
    aw = (w_in.shape[1] - 2 * cw) // 3
    kern = functools.partial(_front_sample_kernel, d=d, aw=aw, cw=cw)
    full = lambda a: pl.BlockSpec(a.shape, lambda i: (0,) * a.ndim)
    args = (x, mods, g_pre, w_in, b_glu, state_t, conv_w, conv_b, ln_g, ln_b)
    out_shapes = (
        jax.ShapeDtypeStruct((rows, aw), F32),
        jax.ShapeDtypeStruct((rows, aw), F32),
        jax.ShapeDtypeStruct((rows, aw), F32),
        jax.ShapeDtypeStruct((rows, cw), F32),
        jax.ShapeDtypeStruct((rows, cw), BF16),
    )
    return pl.pallas_call(
        kern,
        out_shape=out_shapes,
        grid=(1,),
        in_specs=[full(a) for a in args],
        out_specs=tuple(pl.BlockSpec(o.shape, lambda i: (0, 0)) for o in out_shapes),
        compiler_params=pltpu.CompilerParams(
            dimension_semantics=("arbitrary",), vmem_limit_bytes=VMEM_LIMIT_BYTES),
        name="front_sample",
    )(*args)


ROWS = 16
OUT_ROWS = 8
GROUPS = 2


def _head_rows(x_row, sub, lane):
    seg_of_lane = lax.shift_right_logical(lane, 7)
    full = jnp.where(seg_of_lane == jnp.bitwise_and(sub, N_HEADS - 1), x_row, 0.0)
    out = full[:, 0:128]
    for h in range(1, N_HEADS):
        out = out + full[:, h * 128:(h + 1) * 128]
    return out


def _attn_sample_kernel(pt_ref, q_ref, kn_ref, vn_ref, lq1_ref, lk1_ref, lq2_ref, lk2_ref, g_ref,
                        *rest, pp, lam_init):
    k_refs = rest[0:pp]
    v_refs = rest[pp:2 * pp]
    o_ref, m_ref, l_ref, acc_ref = rest[2 * pp:]
    del pt_ref
    j = pl.program_id(1)
    aw = q_ref.shape[-1]
    sub_w = lax.broadcasted_iota(jnp.int32, (ROWS, aw), 0)
    lane_w = lax.broadcasted_iota(jnp.int32, (ROWS, aw), 1)
    sub = lax.broadcasted_iota(jnp.int32, (ROWS, QK_DIM), 0)
    lane = lax.broadcasted_iota(jnp.int32, (ROWS, QK_DIM), 1)
    own_map = lax.shift_right_logical(lane, 6) == lax.shift_right_logical(sub, 2)
    qq = jnp.where(own_map, _head_rows(q_ref[0], sub_w, lane_w), 0.0)
    qq_bf = qq.astype(BF16)

    @pl.when(j == 0)
    def _():
        m_ref[...] = jnp.full(m_ref.shape, NEG, F32)
        l_ref[...] = jnp.zeros(l_ref.shape, F32)
        acc_ref[...] = jnp.zeros(acc_ref.shape, F32)

    n_rows = k_refs[0].shape[1]
    col = lax.broadcasted_iota(jnp.int32, (ROWS, n_rows), 1)
    srow = lax.broadcasted_iota(jnp.int32, (ROWS, n_rows), 0)
    same_head = jnp.bitwise_and(col, N_HEADS - 1) == jnp.bitwise_and(srow, N_HEADS - 1)
    per_group = pp // GROUPS
    for g in range(GROUPS):
        pages = range(g * per_group, (g + 1) * per_group)
        parts = []
        for i in pages:
            s_i = lax.dot_general(qq_bf, k_refs[i][0].astype(BF16), (((1,), (1,)), ((), ())),
                                  preferred_element_type=F32)
            parts.append(jnp.where(same_head, s_i, NEG))
        s = jnp.concatenate(parts, axis=1)
        m_old = m_ref[g]
        m_new = jnp.maximum(m_old, jnp.max(s, axis=1, keepdims=True))
        alpha = jnp.exp2(m_old - m_new)
        p = jnp.exp2(s - m_new)
        l_ref[g] = alpha * l_ref[g] + jnp.sum(p, axis=1, keepdims=True)
        m_ref[g] = m_new
        pv = jnp.zeros((ROWS, V_DIM), F32)
        for n, i in enumerate(pages):
            pv = pv + jnp.dot(p[:, n * n_rows:(n + 1) * n_rows].astype(BF16),
                              v_refs[i][0].astype(BF16), preferred_element_type=F32)
        acc_ref[g] = alpha * acc_ref[g] + pv

    @pl.when(j == pl.num_programs(1) - 1)
    def _():
        k16 = _head_rows(kn_ref[0], sub_w, lane_w)
        v16 = _head_rows(vn_ref[0], sub_w, lane_w)
        s_new = jnp.sum(qq * k16, axis=1, keepdims=True)
        m_fin = s_new
        for g in range(GROUPS):
            m_fin = jnp.maximum(m_fin, m_ref[g])
        p_new = jnp.exp2(s_new - m_fin)
        l_fin = p_new
        acc = p_new * v16
        for g in range(GROUPS):
            alpha = jnp.exp2(m_ref[g] - m_fin)
            l_fin = l_fin + alpha * l_ref[g]
            acc = acc + alpha * acc_ref[g]
        o = acc * (1.0 / l_fin)
        lam = _lam_value(lq1_ref[...], lk1_ref[...], lq2_ref[...], lk2_ref[...], lam_init)
        oc = o - lam * pltpu.roll(o, ROWS - N_HEADS, axis=0)
        y = oc * lax.rsqrt(jnp.mean(oc * oc, axis=-1, keepdims=True) + EPS)
        y = y * g_ref[...] * (1.0 - lam_init)
        o_ref[0] = y[0:OUT_ROWS]


def _attn_sample(page_table, q, k_new, v_new, lam_rows, subln_g, cache_k, cache_v, *, pp, lam_init):
    db, n_pages = page_table.shape
    n_rows = cache_k.shape[1]
    aw = q.shape[-1]
    kern = functools.partial(_attn_sample_kernel, pp=pp, lam_init=lam_init)
    row = pl.BlockSpec((1, 1, aw), lambda b, j, pt: (b, 0, 0))
    small = pl.BlockSpec((1, D_HEAD), lambda b, j, pt: (0, 0))

    def page_spec(i, width):
        return pl.BlockSpec((1, n_rows, width), lambda b, j, pt: (pt[b, j * pp + i], 0, 0))

    grid_spec = pltpu.PrefetchScalarGridSpec(
        num_scalar_prefetch=1,
        grid=(db, n_pages // pp),
        in_specs=[row, row, row, small, small, small, small,
                  pl.BlockSpec((1, V_DIM), lambda b, j, pt: (0, 0))]
                 + [page_spec(i, QK_DIM) for i in range(pp)] + [page_spec(i, V_DIM) for i in range(pp)],
        out_specs=pl.BlockSpec((1, OUT_ROWS, V_DIM), lambda b, j, pt: (b, 0, 0)),
        scratch_shapes=[pltpu.VMEM((GROUPS, ROWS, 1), F32), pltpu.VMEM((GROUPS, ROWS, 1), F32),
                        pltpu.VMEM((GROUPS, ROWS, V_DIM), F32)],
    )
    assert pp % GROUPS == 0
    out = pl.pallas_call(
        kern,
        out_shape=jax.ShapeDtypeStruct((db, OUT_ROWS, V_DIM), F32),
        grid_spec=grid_spec,
        compiler_params=pltpu.CompilerParams(
            dimension_semantics=("arbitrary", "arbitrary"),
            vmem_limit_bytes=VMEM_LIMIT_BYTES),
        name="attn_sample",
    )(page_table, q.reshape(db, 1, aw), k_new.reshape(db, 1, aw), v_new.reshape(db, 1, aw),
      *lam_rows, subln_g, *([cache_k] * pp), *([cache_v] * pp))
    return out[:, 0:N_HEADS, :].reshape(db, N_HEADS * V_DIM)


def _tile(n, pref):
    t = min(n, pref)
    while n % t:
        t //= 2
    return t


def kernel(x_prompt, x_sample, c_prompt, c_sample, cache_k, cache_v, state_conv, page_table, w_ada, b_ada, g_pre_mix, g_post_mix, g_pre_ffn, g_post_ffn, w_in, b_glu, conv_w, conv_b, conv_ln_g, conv_ln_b, lambda_q1, lambda_k1, lambda_q2, lambda_k2, subln_g, w_out, w_gate, w_up, w_down):
    depth = w_in.shape[0]
    assert depth == 1, "single-layer step"
    l = 0
    lam_init = 0.8 - 0.6 * math.exp(-0.3 * l)
    b, s, d = x_prompt.shape
    db = x_sample.shape[0]
    assert x_sample.shape[1] == 1
    cw = conv_w.shape[-1]
    aw = N_HEADS * V_DIM
    n_pool, page = cache_k.shape[1], cache_k.shape[2]

    row = lambda a: a[l].reshape(1, -1)
    lam_rows = (row(lambda_q1), row(lambda_k1), row(lambda_q2), row(lambda_k2))
    g_sub = row(subln_g)

    mods = _ada(jnp.concatenate([c_prompt, c_sample], axis=0), w_ada[l], b_ada[l])
    mods_p = mods[:b].reshape(b, 1, N_ADA * d)
    mods_s = mods[b:]

    w_in_bf = w_in[l].astype(BF16)
    w_kv = w_in_bf[:, aw:3 * aw]
    w_glu = w_in_bf[:, 3 * aw:]
    w_qvt = jnp.concatenate([w_in_bf[:, 0:aw], w_in_bf[:, 2 * aw:3 * aw]], axis=1).T
    w_out_bf = w_out[l].astype(BF16)
    w_gate_bf = w_gate[l].astype(BF16)
    w_up_bf = w_up[l].astype(BF16)
    w_down_bf = w_down[l].astype(BF16)

    tm = _tile(s, 512)
    k_p, v_p, kbf, qt, vt, yc_p, u_tail = _front_prompt(
        x_prompt, mods_p, row(g_pre_mix), w_glu, w_kv, w_qvt, row(b_glu),
        conv_w[l], row(conv_b), row(conv_ln_g), row(conv_ln_b), tm=tm)
    tq = _tile(s, 512)
    ya_p = _attn_prompt(qt, kbf, vt, lam_rows, g_sub, tq=tq, lam_init=lam_init)
    y_prompt = _post(ya_p, yc_p, x_prompt, mods_p, row(g_post_mix), row(g_pre_ffn), row(g_post_ffn),
                     w_out_bf, w_gate_bf, w_up_bf, w_down_bf, tm=tm)
    conv_p = u_tail[:, HALO - CONV_STATE:, :]

    state = state_conv[l]
    q_s, k_s, v_s, u_s, yc_s = _front_sample(
        x_sample.reshape(db, d), mods_s, row(g_pre_mix), w_in_bf, row(b_glu),
        jnp.transpose(state, (1, 0, 2)), conv_w[l], row(conv_b), row(conv_ln_g), row(conv_ln_b))
    pp = _tile(page_table.shape[1], 16)
    ya_s = _attn_sample(page_table + l * n_pool, q_s, k_s, v_s, lam_rows, g_sub,
                        cache_k.reshape(depth * n_pool, page * N_HEADS, QK_DIM),
                        cache_v.reshape(depth * n_pool, page * N_HEADS, V_DIM),
                        pp=pp, lam_init=lam_init)
    y_sample = _post(ya_s.reshape(1, db, aw), yc_s.reshape(1, db, cw), x_sample.reshape(1, db, d),
                     mods_s.reshape(1, db, N_ADA * d), row(g_post_mix), row(g_pre_ffn),
                     row(g_post_ffn), w_out_bf, w_gate_bf, w_up_bf, w_down_bf, tm=_tile(db, 512))
    conv_s = jnp.concatenate([state[:, 1:, :], u_s[:, None, :]], axis=1)

    return (y_prompt,
            y_sample.reshape(db, 1, d),
            k_p.reshape(1, b, s, N_HEADS, QK_DIM),
            v_p.reshape(1, b, s, N_HEADS, V_DIM),
            conv_p[None],
            k_s.reshape(1, db, 1, N_HEADS, QK_DIM),
            v_s.reshape(1, db, 1, N_HEADS, V_DIM),
            conv_s[None])
```

```python
import functools
import math

import jax
import jax.numpy as jnp
from jax import lax
from jax.experimental import pallas as pl
from jax.experimental.pallas import tpu as pltpu

F32 = jnp.float32
BF16 = jnp.bfloat16

N_HEADS = 4
D_HEAD = 64
QK_DIM = 2 * D_HEAD
V_DIM = 128
CONV_K = 31
CONV_STATE = CONV_K - 1
N_ADA = 6
EPS = 1e-6
NEG = -1e30
Q_SCALE = math.log2(math.e) / math.sqrt(D_HEAD)

VMEM_LIMIT_BYTES = 56 * 1024 * 1024
SUBLANES = 8
LANES = 128
BF16_ROWS = 16
ACC_ROWS = V_DIM + SUBLANES
HALO = 32


def _rms(x, g):
    return x * lax.rsqrt(jnp.mean(x * x, axis=-1, keepdims=True) + EPS) * g


def _silu(x):
    return x * jax.nn.sigmoid(x)


def _const_spec(shape):
    zeros = (0,) * len(shape)
    return pl.BlockSpec(shape, lambda *_: zeros, pipeline_mode=pl.Buffered(1))


def _lam_value(lq1, lk1, lq2, lk2, lam_init):
    a = jnp.sum(lq1 * lk1, axis=-1, keepdims=True)
    b = jnp.sum(lq2 * lk2, axis=-1, keepdims=True)
    return jnp.exp(a) - jnp.exp(b) + lam_init


def _ada_kernel(c_ref, w_ref, b_ref, o_ref):
    c = c_ref[...]
    a = _silu(c).astype(BF16)
    o_ref[...] = jnp.dot(a, w_ref[...].astype(BF16), preferred_element_type=F32) + b_ref[...]


def _ada(c_all, w_ada, b_ada):
    rows, d = c_all.shape
    n = w_ada.shape[1]
    tn = 1024 if n % 1024 == 0 else n
    return pl.pallas_call(
        _ada_kernel,
        out_shape=jax.ShapeDtypeStruct((rows, n), F32),
        grid=(n // tn,),
        in_specs=[
            pl.BlockSpec((rows, d), lambda j: (0, 0)),
            pl.BlockSpec((d, tn), lambda j: (0, j)),
            pl.BlockSpec((1, tn), lambda j: (0, j)),
        ],
        out_specs=pl.BlockSpec((rows, tn), lambda j: (0, j)),
        compiler_params=pltpu.CompilerParams(dimension_semantics=("arbitrary",)),
        name="ada_mod",
    )(c_all, w_ada, b_ada.reshape(1, n))


def _conv_ln_swish(ext_ref, sh_ref, y_ref, w_ref, cb_ref, lg_ref, lb_ref, yc_ref, *, tm, rc):
    n_sh = sh_ref.shape[1]
    for r in range(1, SUBLANES):
        sh_ref[r - 1] = ext_ref[r:r + n_sh, :]
    off = HALO - CONV_STATE
    for q in range(ext_ref.shape[1] // LANES):
        lanes = slice(q * LANES, (q + 1) * LANES)
        taps = [w_ref[j:j + 1, lanes] for j in range(CONV_K)]
        for c in range(tm // rc):
            acc = jnp.zeros((rc, LANES), F32)
            for j in range(CONV_K):
                a, r = divmod(off + j, SUBLANES)
                r0 = c * rc + a * SUBLANES
                win = ext_ref[r0:r0 + rc, lanes] if r == 0 else sh_ref[r - 1, r0:r0 + rc, lanes]
                acc = acc + taps[j] * win
            y_ref[c * rc:(c + 1) * rc, lanes] = acc
    for c in range(tm // rc):
        y = y_ref[c * rc:(c + 1) * rc, :] + cb_ref[...]
        mu = jnp.mean(y, axis=-1, keepdims=True)
        yc = y - mu
        yn = yc * lax.rsqrt(jnp.mean(yc * yc, axis=-1, keepdims=True) + EPS)
        yn = yn * lg_ref[...] + lb_ref[...]
        yc_ref[0, c * rc:(c + 1) * rc, :] = _silu(yn).astype(yc_ref.dtype)


def _front_prompt_kernel(x_ref, mods_ref, g_ref, wglu_ref, wkv_ref, wqvt_ref, bglu_ref,
                         cw_ref, cb_ref, lg_ref, lb_ref,
                         k_ref, v_ref, kbf_ref, qt_ref, vt_ref, yc_ref, tail_ref,
                         ext_ref, sh_ref, y_ref, *, d, aw, cw, tm, rc):
    si = pl.program_id(1)
    x = x_ref[0]
    shift = mods_ref[0, :, 0:d]
    scale = mods_ref[0, :, d:2 * d]
    h = _rms(x, g_ref[...]) * (1.0 + scale) + shift
    hb = h.astype(BF16)

    ga = jnp.dot(hb, wglu_ref[...], preferred_element_type=F32) + bglu_ref[...]
    u = ga[:, 0:cw] * jax.nn.sigmoid(ga[:, cw:])

    @pl.when(si == 0)
    def _():
        ext_ref[0:HALO, :] = jnp.zeros((HALO, cw), F32)

    @pl.when(si > 0)
    def _():
        ext_ref[0:HALO, :] = ext_ref[tm:tm + HALO, :]

    ext_ref[HALO:HALO + tm, :] = u
    tail_ref[0] = u[tm - HALO:tm, :]
    _conv_ln_swish(ext_ref, sh_ref, y_ref, cw_ref, cb_ref, lg_ref, lb_ref, yc_ref, tm=tm, rc=rc)

    z = jnp.dot(hb, wkv_ref[...], preferred_element_type=F32)
    k = z[:, 0:aw]
    v = z[:, aw:2 * aw]
    for hd in range(N_HEADS):
        k_ref[0, pl.ds(hd, tm, stride=N_HEADS), :] = k[:, hd * QK_DIM:(hd + 1) * QK_DIM]
        v_ref[0, pl.ds(hd, tm, stride=N_HEADS), :] = v[:, hd * V_DIM:(hd + 1) * V_DIM]
    kbf_ref[0] = k.astype(BF16)
    zt = lax.dot_general(wqvt_ref[...], hb, (((1,), (1,)), ((), ())),
                         preferred_element_type=F32)
    qt_ref[0] = (zt[0:aw] * Q_SCALE).astype(BF16)
    vt_ref[0] = zt[aw:].astype(BF16)


def _front_prompt(x, mods, g_pre, w_glu, w_kv, w_qvt, b_glu, conv_w, conv_b, ln_g, ln_b, *, tm):
    b, s, d = x.shape
    aw = w_qvt.shape[0] // 2
    cw = b_glu.shape[-1] // 2
    assert tm >= HALO and tm % SUBLANES == 0
    rc = min(128, tm)
    kern = functools.partial(_front_prompt_kernel, d=d, aw=aw, cw=cw, tm=tm, rc=rc)
    row_blk = lambda w: pl.BlockSpec((1, tm, w), lambda i, j: (i, j, 0))
    head_blk = lambda w: pl.BlockSpec((1, tm * N_HEADS, w), lambda i, j: (i, j, 0))
    col_blk = pl.BlockSpec((1, aw, tm), lambda i, j: (i, 0, j))
    return pl.pallas_call(
        kern,
        out_shape=(
            jax.ShapeDtypeStruct((b, s * N_HEADS, QK_DIM), F32),
            jax.ShapeDtypeStruct((b, s * N_HEADS, V_DIM), F32),
            jax.ShapeDtypeStruct((b, s, aw), BF16),
            jax.ShapeDtypeStruct((b, aw, s), BF16),
            jax.ShapeDtypeStruct((b, aw, s), BF16),
            jax.ShapeDtypeStruct((b, s, cw), BF16),
            jax.ShapeDtypeStruct((b, HALO, cw), F32),
        ),
        grid=(b, s // tm),
        in_specs=[
            row_blk(d),
            pl.BlockSpec((1, 1, N_ADA * d), lambda i, j: (i, 0, 0)),
            _const_spec((1, d)),
            _const_spec(w_glu.shape),
            _const_spec(w_kv.shape),
            _const_spec(w_qvt.shape),
            _const_spec((1, 2 * cw)),
            _const_spec((CONV_K, cw)),
            _const_spec((1, cw)),
            _const_spec((1, cw)),
            _const_spec((1, cw)),
        ],
        out_specs=(head_blk(QK_DIM), head_blk(V_DIM), row_blk(aw), col_blk, col_blk, row_blk(cw),
                   pl.BlockSpec((1, HALO, cw), lambda i, j: (i, 0, 0))),
        scratch_shapes=[pltpu.VMEM((HALO + tm, cw), F32),
                        pltpu.VMEM((SUBLANES - 1, HALO + tm - SUBLANES, cw), F32),
                        pltpu.VMEM((tm, cw), F32)],
        compiler_params=pltpu.CompilerParams(
            dimension_semantics=("arbitrary", "arbitrary"),
            vmem_limit_bytes=VMEM_LIMIT_BYTES),
        name="front_prompt",
    )(x, mods, g_pre, w_glu, w_kv, w_qvt, b_glu, conv_w, conv_b, ln_g, ln_b)


def _attn_prompt_kernel(qt_ref, k_ref, vt_ref, lq1_ref, lk1_ref, lq2_ref, lk2_ref, g_ref,
                        o_ref, w_ref, sa_ref, sb_ref, acc_ref, *, tq, tk, lam_init):
    per_q = tq // tk
    qi = pl.program_id(2)
    qt = qt_ref[0]
    row = lax.broadcasted_iota(jnp.int32, qt.shape, 0)
    zero = jnp.zeros_like(qt)
    w_ref[0] = jnp.where(row < D_HEAD, qt, zero)
    w_ref[1] = jnp.where(row >= D_HEAD, qt, zero)
    acc_ref[...] = jnp.zeros_like(acc_ref)
    ones_rows = (lax.broadcasted_iota(jnp.int32, (BF16_ROWS, tk), 0) == 0).astype(BF16)

    def key_slice(chunk):
        start = chunk * tk
        return pl.ds(start if isinstance(start, int) else pl.multiple_of(start, tk), tk)

    def scores(chunk, s_ref, lo=0):
        kblk = k_ref[0, key_slice(chunk), :]
        for c in range(2):
            s_ref[c, :, lo:] = jnp.dot(kblk, w_ref[c, :, lo:], preferred_element_type=F32)

    def softmax_pv(chunk, s_ref, carry, mask, lo=0):
        vblk = jnp.concatenate([vt_ref[0, :, key_slice(chunk)], ones_rows], axis=0)
        out = []
        for c in range(2):
            m = carry[c][:, lo:]
            s = s_ref[c, :, lo:]
            if mask is not None:
                s = jnp.where(mask[:, lo:], s, NEG)
            m_new = jnp.maximum(m, jnp.max(s, axis=0, keepdims=True))
            alpha = jnp.exp2(m - m_new)
            p = jnp.exp2((s - m_new).astype(BF16))
            pv = jnp.dot(vblk, p, preferred_element_type=F32)
            acc_ref[c, :, lo:] = alpha * acc_ref[c, :, lo:] + pv[0:ACC_ROWS]
            out.append(m_new if lo == 0 else jnp.concatenate([carry[c][:, :lo], m_new], axis=1))
        return tuple(out)

    init = jnp.full((1, tq), NEG, F32)
    scores(0, sa_ref)

    def pair(j, carry):
        scores(2 * j + 1, sb_ref)
        carry = softmax_pv(2 * j, sa_ref, carry, None)
        scores(2 * j + 2, sa_ref)
        return softmax_pv(2 * j + 1, sb_ref, carry, None)

    carry = lax.fori_loop(0, qi * (per_q // 2), pair, (init, init))
    key_rel = lax.broadcasted_iota(jnp.int32, (tk, tq), 0)
    q_rel = lax.broadcasted_iota(jnp.int32, (tk, tq), 1)
    base = qi * per_q
    bufs = (sa_ref, sb_ref)
    for c in range(per_q):
        if c + 1 < per_q:
            scores(base + c + 1, bufs[(c + 1) % 2], lo=(c + 1) * tk)
        carry = softmax_pv(base + c, bufs[c % 2], carry, key_rel + c * tk <= q_rel, lo=c * tk)

    lam = _lam_value(lq1_ref[...], lk1_ref[...], lq2_ref[...], lk2_ref[...], lam_init)
    l1 = acc_ref[0, V_DIM:V_DIM + 1, :]
    l2 = acc_ref[1, V_DIM:V_DIM + 1, :]
    ot = acc_ref[0, 0:V_DIM, :] * (1.0 / l1) - lam * (acc_ref[1, 0:V_DIM, :] * (1.0 / l2))
    ot = ot * lax.rsqrt(jnp.mean(ot * ot, axis=0, keepdims=True) + EPS)
    o = ot.T * g_ref[...] * (1.0 - lam_init)
    o_ref[0] = o.astype(o_ref.dtype)


def _attn_prompt(qt, kbf, vt, lam_rows, subln_g, *, tq, tk, lam_init):
    b, aw, s = qt.shape
    assert tq % (2 * tk) == 0
    kern = functools.partial(_attn_prompt_kernel, tq=tq, tk=tk, lam_init=lam_init)
    small = _const_spec((1, D_HEAD))
    return pl.pallas_call(
        kern,
        out_shape=jax.ShapeDtypeStruct((b, s, aw), BF16),
        grid=(b, N_HEADS, s // tq),
        in_specs=[
            pl.BlockSpec((1, QK_DIM, tq), lambda i, h, j: (i, h, j)),
            pl.BlockSpec((1, s, QK_DIM), lambda i, h, j: (i, 0, h)),
            pl.BlockSpec((1, V_DIM, s), lambda i, h, j: (i, h, 0)),
            small, small, small, small,
            _const_spec((1, V_DIM)),
        ],
        out_specs=pl.BlockSpec((1, tq, V_DIM), lambda i, h, j: (i, j, h)),
        scratch_shapes=[pltpu.VMEM((2, QK_DIM, tq), BF16),
                        pltpu.VMEM((2, tk, tq), F32), pltpu.VMEM((2, tk, tq), F32),
                        pltpu.VMEM((2, ACC_ROWS, tq), F32)],
        compiler_params=pltpu.CompilerParams(
            dimension_semantics=("arbitrary", "arbitrary", "arbitrary"),
            vmem_limit_bytes=VMEM_LIMIT_BYTES),
        name="attn_prompt",
    )(qt, kbf, vt, *lam_rows, subln_g)


def _post_kernel(ya_ref, yc_ref, x_ref, mods_ref, gpm_ref, gpf_ref, gpo_ref,
                 wout_ref, wg_ref, wu_ref, wd_ref, y_ref, acc_ref, *, d, aw, ff_chunks):
    gate1 = mods_ref[0, :, 2 * d:3 * d]
    shift2 = mods_ref[0, :, 3 * d:4 * d]
    scale2 = mods_ref[0, :, 4 * d:5 * d]
    gate2 = mods_ref[0, :, 5 * d:6 * d]
    m = jnp.dot(ya_ref[0].astype(BF16), wout_ref[0:aw, :], preferred_element_type=F32)
    m = m + jnp.dot(yc_ref[0].astype(BF16), wout_ref[aw:, :], preferred_element_type=F32)
    x1 = x_ref[0] + gate1 * _rms(m, gpm_ref[...])
    hb = (_rms(x1, gpf_ref[...]) * (1.0 + scale2) + shift2).astype(BF16)
    for n, (c0, c1) in enumerate(ff_chunks):
        g = jnp.dot(hb, wg_ref[:, c0:c1], preferred_element_type=F32)
        u = jnp.dot(hb, wu_ref[:, c0:c1], preferred_element_type=F32)
        a = (_silu(g) * u).astype(BF16)
        part = jnp.dot(a, wd_ref[c0:c1, :], preferred_element_type=F32)
        if n == 0:
            acc_ref[...] = part
        else:
            acc_ref[...] += part
    y_ref[0] = x1 + gate2 * _rms(acc_ref[...], gpo_ref[...])


def _post(ya, yc, x, mods, g_post_mix, g_pre_ffn, g_post_ffn, w_out, w_gate, w_up, w_down, *, tm):
    b, s, d = x.shape
    aw = ya.shape[-1]
    cw = yc.shape[-1]
    dff = w_gate.shape[1]
    step = 512
    ff_chunks = tuple((c, min(c + step, dff)) for c in range(0, dff, step))
    mod_rows = mods.shape[1]
    mt = 1 if mod_rows == 1 else tm
    kern = functools.partial(_post_kernel, d=d, aw=aw, ff_chunks=ff_chunks)
    return pl.pallas_call(
        kern,
        out_shape=jax.ShapeDtypeStruct((b, s, d), F32),
        grid=(b, s // tm),
        in_specs=[
            pl.BlockSpec((1, tm, aw), lambda i, j: (i, j, 0)),
            pl.BlockSpec((1, tm, cw), lambda i, j: (i, j, 0)),
            pl.BlockSpec((1, tm, d), lambda i, j: (i, j, 0)),
            pl.BlockSpec((1, mt, N_ADA * d),
                         (lambda i, j: (i, 0, 0)) if mod_rows == 1 else (lambda i, j: (i, j, 0))),
            _const_spec((1, d)), _const_spec((1, d)), _const_spec((1, d)),
            _const_spec(w_out.shape), _const_spec(w_gate.shape),
            _const_spec(w_up.shape), _const_spec(w_down.shape),
        ],
        out_specs=pl.BlockSpec((1, tm, d), lambda i, j: (i, j, 0)),
        scratch_shapes=[pltpu.VMEM((tm, d), F32)],
        compiler_params=pltpu.CompilerParams(
            dimension_semantics=("arbitrary", "arbitrary"),
            vmem_limit_bytes=VMEM_LIMIT_BYTES),
        name="post_ffn",
    )(ya, yc, x, mods, g_post_mix, g_pre_ffn, g_post_ffn, w_out, w_gate, w_up, w_down)


def _front_sample_kernel(x_ref, mods_ref, g_ref, win_ref, bglu_ref, st_ref, cw_ref, cb_ref,
                         lg_ref, lb_ref, q_ref, k_ref, v_ref, u_ref, yc_ref, *, d, aw, cw):
    x = x_ref[...]
    shift = mods_ref[:, 0:d]
    scale = mods_ref[:, d:2 * d]
    hb = (_rms(x, g_ref[...]) * (1.0 + scale) + shift).astype(BF16)
    z = jnp.dot(hb, win_ref[...], preferred_element_type=F32)
    q_ref[...] = z[:, 0:aw] * Q_SCALE
    k_ref[...] = z[:, aw:2 * aw]
    v_ref[...] = z[:, 2 * aw:3 * aw]
    ga = z[:, 3 * aw:] + bglu_ref[...]
    u = ga[:, 0:cw] * jax.nn.sigmoid(ga[:, cw:])
    u_ref[...] = u
    y = cw_ref[CONV_STATE:CONV_K, :] * u + cb_ref[...]
    for j in range(CONV_STATE):
        y = y + cw_ref[j:j + 1, :] * st_ref[j]
    mu = jnp.mean(y, axis=-1, keepdims=True)
    yc = y - mu
    yn = yc * lax.rsqrt(jnp.mean(yc * yc, axis=-1, keepdims=True) + EPS)
    yc_ref[...] = _silu(yn * lg_ref[...] + lb_ref[...]).astype(yc_ref.dtype)


def _front_sample(x, mods, g_pre, w_in, b_glu, state_t, conv_w, conv_b, ln_g, ln_b):
    rows, d = x.shape
    cw = conv_w.shape[1]
    aw = (w_in.shape[1] - 2 * cw) // 3
    kern = functools.partial(_front_sample_kernel, d=d, aw=aw, cw=cw)
    full = lambda a: pl.BlockSpec(a.shape, lambda i: (0,) * a.ndim)
    args = (x, mods, g_pre, w_in, b_glu, state_t, conv_w, conv_b, ln_g, ln_b)
    out_shapes = (
        jax.ShapeDtypeStruct((rows, aw), F32),
        jax.ShapeDtypeStruct((rows, aw), F32),
        jax.ShapeDtypeStruct((rows, aw), F32),
        jax.ShapeDtypeStruct((rows, cw), F32),
        jax.ShapeDtypeStruct((rows, cw), BF16),
    )
    return pl.pallas_call(
        kern,
        out_shape=out_shapes,
        grid=(1,),
        in_specs=[full(a) for a in args],
        out_specs=tuple(pl.BlockSpec(o.shape, lambda i: (0, 0)) for o in out_shapes),
        compiler_params=pltpu.CompilerParams(
            dimension_semantics=("arbitrary",), vmem_limit_bytes=VMEM_LIMIT_BYTES),
        name="front_sample",
    )(*args)


ROWS = 16
OUT_ROWS = 8
GROUPS = 2
N_SLOTS = 3


def _head_rows(x_row, sub, lane):
    seg_of_lane = lax.shift_right_logical(lane, 7)
    full = jnp.where(seg_of_lane == jnp.bitwise_and(sub, N_HEADS - 1), x_row, 0.0)
    out = full[:, 0:128]
    for h in range(1, N_HEADS):
        out = out + full[:, h * 128:(h + 1) * 128]
    return out


def _attn_sample_kernel(pt_ref, q_ref, kn_ref, vn_ref, lq1_ref, lk1_ref, lq2_ref, lk2_ref, g_ref,
                        ck_ref, cv_ref, o_ref, kbuf_ref, vbuf_ref, sem_ref, m_ref, l_ref, acc_ref,
                        *, pp, n_steps, total_steps, lam_init):
    b = pl.program_id(0)
    j = pl.program_id(1)
    t = b * n_steps + j

    def page_copies(step, slot):
        seq = step // n_steps
        first = (step - seq * n_steps) * pp
        copies = []
        for i in range(pp):
            page = pt_ref[seq, first + i]
            copies.append(pltpu.make_async_copy(ck_ref.at[page], kbuf_ref.at[slot, i], sem_ref.at[0, slot]))
            copies.append(pltpu.make_async_copy(cv_ref.at[page], vbuf_ref.at[slot, i], sem_ref.at[1, slot]))
        return copies

    @pl.when(t == 0)
    def _():
        for s in range(N_SLOTS - 1):
            for c in page_copies(s, s):
                c.start()

    ahead = t + (N_SLOTS - 1)

    @pl.when(ahead < total_steps)
    def _():
        for c in page_copies(ahead, lax.rem(ahead, N_SLOTS)):
            c.start()

    slot = lax.rem(t, N_SLOTS)
    for c in page_copies(t, slot):
        c.wait()

    aw = q_ref.shape[-1]
    sub_w = lax.broadcasted_iota(jnp.int32, (ROWS, aw), 0)
    lane_w = lax.broadcasted_iota(jnp.int32, (ROWS, aw), 1)
    sub = lax.broadcasted_iota(jnp.int32, (ROWS, QK_DIM), 0)
    lane = lax.broadcasted_iota(jnp.int32, (ROWS, QK_DIM), 1)
    own_map = lax.shift_right_logical(lane, 6) == lax.shift_right_logical(sub, 2)
    qq = jnp.where(own_map, _head_rows(q_ref[0], sub_w, lane_w), 0.0)
    qq_bf = qq.astype(BF16)

    @pl.when(j == 0)
    def _():
        m_ref[...] = jnp.full(m_ref.shape, NEG, F32)
        l_ref[...] = jnp.zeros(l_ref.shape, F32)
        acc_ref[...] = jnp.zeros(acc_ref.shape, F32)

    n_rows = kbuf_ref.shape[2]
    col = lax.broadcasted_iota(jnp.int32, (ROWS, n_rows), 1)
    srow = lax.broadcasted_iota(jnp.int32, (ROWS, n_rows), 0)
    same_head = jnp.bitwise_and(col, N_HEADS - 1) == jnp.bitwise_and(srow, N_HEADS - 1)
    per_group = pp // GROUPS
    for g in range(GROUPS):
        pages = range(g * per_group, (g + 1) * per_group)
        parts = []
        for i in pages:
            s_i = lax.dot_general(qq_bf, kbuf_ref[slot, i].astype(BF16), (((1,), (1,)), ((), ())),
                                  preferred_element_type=F32)
            parts.append(jnp.where(same_head, s_i, NEG))
        s = jnp.concatenate(parts, axis=1)
        m_old = m_ref[g]
        m_new = jnp.maximum(m_old, jnp.max(s, axis=1, keepdims=True))
        alpha = jnp.exp2(m_old - m_new)
        p = jnp.exp2(s - m_new)
        l_ref[g] = alpha * l_ref[g] + jnp.sum(p, axis=1, keepdims=True)
        m_ref[g] = m_new
        pv = jnp.zeros((ROWS, V_DIM), F32)
        for n, i in enumerate(pages):
            pv = pv + jnp.dot(p[:, n * n_rows:(n + 1) * n_rows].astype(BF16),
                              vbuf_ref[slot, i].astype(BF16), preferred_element_type=F32)
        acc_ref[g] = alpha * acc_ref[g] + pv

    @pl.when(j == n_steps - 1)
    def _():
        k16 = _head_rows(kn_ref[0], sub_w, lane_w)
        v16 = _head_rows(vn_ref[0], sub_w, lane_w)
        s_new = jnp.sum(qq * k16, axis=1, keepdims=True)
        m_fin = s_new
        for g in range(GROUPS):
            m_fin = jnp.maximum(m_fin, m_ref[g])
        p_new = jnp.exp2(s_new - m_fin)
        l_fin = p_new
        acc = p_new * v16
        for g in range(GROUPS):
            alpha = jnp.exp2(m_ref[g] - m_fin)
            l_fin = l_fin + alpha * l_ref[g]
            acc = acc + alpha * acc_ref[g]
        o = acc * (1.0 / l_fin)
        lam = _lam_value(lq1_ref[...], lk1_ref[...], lq2_ref[...], lk2_ref[...], lam_init)
        oc = o - lam * pltpu.roll(o, ROWS - N_HEADS, axis=0)
        y = oc * lax.rsqrt(jnp.mean(oc * oc, axis=-1, keepdims=True) + EPS)
        y = y * g_ref[...] * (1.0 - lam_init)
        o_ref[0] = y[0:OUT_ROWS]


def _attn_sample(page_table, q, k_new, v_new, lam_rows, subln_g, cache_k, cache_v, *, pp, lam_init):
    db, n_pages = page_table.shape
    n_rows = cache_k.shape[1]
    aw = q.shape[-1]
    n_steps = n_pages // pp
    total_steps = db * n_steps
    assert pp % GROUPS == 0 and total_steps >= N_SLOTS
    kern = functools.partial(_attn_sample_kernel, pp=pp, n_steps=n_steps, total_steps=total_steps,
                             lam_init=lam_init)
    row = pl.BlockSpec((1, 1, aw), lambda b, j, pt: (b, 0, 0))
    small = pl.BlockSpec((1, D_HEAD), lambda b, j, pt: (0, 0))
    hbm = pl.BlockSpec(memory_space=pl.ANY)

    grid_spec = pltpu.PrefetchScalarGridSpec(
        num_scalar_prefetch=1,
        grid=(db, n_steps),
        in_specs=[row, row, row, small, small, small, small,
                  pl.BlockSpec((1, V_DIM), lambda b, j, pt: (0, 0)), hbm, hbm],
        out_specs=pl.BlockSpec((1, OUT_ROWS, V_DIM), lambda b, j, pt: (b, 0, 0)),
        scratch_shapes=[pltpu.VMEM((N_SLOTS, pp, n_rows, QK_DIM), F32),
                        pltpu.VMEM((N_SLOTS, pp, n_rows, V_DIM), F32),
                        pltpu.SemaphoreType.DMA((2, N_SLOTS)),
                        pltpu.VMEM((GROUPS, ROWS, 1), F32), pltpu.VMEM((GROUPS, ROWS, 1), F32),
                        pltpu.VMEM((GROUPS, ROWS, V_DIM), F32)],
    )
    out = pl.pallas_call(
        kern,
        out_shape=jax.ShapeDtypeStruct((db, OUT_ROWS, V_DIM), F32),
        grid_spec=grid_spec,
        compiler_params=pltpu.CompilerParams(
            dimension_semantics=("arbitrary", "arbitrary"),
            vmem_limit_bytes=VMEM_LIMIT_BYTES),
        name="attn_sample",
    )(page_table, q.reshape(db, 1, aw), k_new.reshape(db, 1, aw), v_new.reshape(db, 1, aw),
      *lam_rows, subln_g, cache_k, cache_v)
    return out[:, 0:N_HEADS, :].reshape(db, N_HEADS * V_DIM)


def _tile(n, pref):
    t = min(n, pref)
    while n % t:
        t //= 2
    return t


def kernel(x_prompt, x_sample, c_prompt, c_sample, cache_k, cache_v, state_conv, page_table, w_ada, b_ada, g_pre_mix, g_post_mix, g_pre_ffn, g_post_ffn, w_in, b_glu, conv_w, conv_b, conv_ln_g, conv_ln_b, lambda_q1, lambda_k1, lambda_q2, lambda_k2, subln_g, w_out, w_gate, w_up, w_down):
    depth = w_in.shape[0]
    assert depth == 1, "single-layer step"
    l = 0
    lam_init = 0.8 - 0.6 * math.exp(-0.3 * l)
    b, s, d = x_prompt.shape
    db = x_sample.shape[0]
    assert x_sample.shape[1] == 1
    cw = conv_w.shape[-1]
    aw = N_HEADS * V_DIM
    n_pool, page = cache_k.shape[1], cache_k.shape[2]

    row = lambda a: a[l].reshape(1, -1)
    lam_rows = (row(lambda_q1), row(lambda_k1), row(lambda_q2), row(lambda_k2))
    g_sub = row(subln_g)

    mods = _ada(jnp.concatenate([c_prompt, c_sample], axis=0), w_ada[l], b_ada[l])
    mods_p = mods[:b].reshape(b, 1, N_ADA * d)
    mods_s = mods[b:]

    w_in_bf = w_in[l].astype(BF16)
    w_kv = w_in_bf[:, aw:3 * aw]
    w_glu = w_in_bf[:, 3 * aw:]
    w_qvt = jnp.concatenate([w_in_bf[:, 0:aw], w_in_bf[:, 2 * aw:3 * aw]], axis=1).T
    w_out_bf = w_out[l].astype(BF16)
    w_gate_bf = w_gate[l].astype(BF16)
    w_up_bf = w_up[l].astype(BF16)
    w_down_bf = w_down[l].astype(BF16)

    state = state_conv[l]
    q_s, k_s, v_s, u_s, yc_s = _front_sample(
        x_sample.reshape(db, d), mods_s, row(g_pre_mix), w_in_bf, row(b_glu),
        jnp.transpose(state, (1, 0, 2)), conv_w[l], row(conv_b), row(conv_ln_g), row(conv_ln_b))
    pp = _tile(page_table.shape[1], 16)
    ya_s = _attn_sample(page_table + l * n_pool, q_s, k_s, v_s, lam_rows, g_sub,
                        cache_k.reshape(depth * n_pool, page * N_HEADS, QK_DIM),
                        cache_v.reshape(depth * n_pool, page * N_HEADS, V_DIM),
                        pp=pp, lam_init=lam_init)
    y_sample = _post(ya_s.reshape(1, db, aw), yc_s.reshape(1, db, cw), x_sample.reshape(1, db, d),
                     mods_s.reshape(1, db, N_ADA * d), row(g_post_mix), row(g_pre_ffn),
                     row(g_post_ffn), w_out_bf, w_gate_bf, w_up_bf, w_down_bf, tm=_tile(db, 512))
    conv_s = jnp.concatenate([state[:, 1:, :], u_s[:, None, :]], axis=1)

    tm = _tile(s, 512)
    k_p, v_p, kbf, qt, vt, yc_p, u_tail = _front_prompt(
        x_prompt, mods_p, row(g_pre_mix), w_glu, w_kv, w_qvt, row(b_glu),
        conv_w[l], row(conv_b), row(conv_ln_g), row(conv_ln_b), tm=tm)
    tq = _tile(s, 1024)
    ya_p = _attn_prompt(qt, kbf, vt, lam_rows, g_sub, tq=tq, tk=tq // 4, lam_init=lam_init)
    y_prompt = _post(ya_p, yc_p, x_prompt, mods_p, row(g_post_mix), row(g_pre_ffn), row(g_post_ffn),
                     w_out_bf, w_gate_bf, w_up_bf, w_down_bf, tm=tm)
    conv_p = u_tail[:, HALO - CONV_STATE:, :]

    return (y_prompt,
            y_sample.reshape(db, 1, d),
            k_p.reshape(1, b, s, N_HEADS, QK_DIM),
            v_p.reshape(1, b, s, N_HEADS, V_DIM),
            conv_p[None],
            k_s.reshape(1, db, 1, N_HEADS, QK_DIM),
            v_s.reshape(1, db, 1, N_HEADS, V_DIM),
            conv_s[None])
```

```python
import functools
import math

import jax
import jax.numpy as jnp
from jax import lax
from jax.experimental import pallas as pl
from jax.experimental.pallas import tpu as pltpu

F32 = jnp.float32
BF16 = jnp.bfloat16

N_HEADS = 4
D_HEAD = 64
QK_DIM = 2 * D_HEAD
V_DIM = 128
CONV_K = 31
CONV_STATE = CONV_K - 1
N_ADA = 6
EPS = 1e-6
NEG = -1e30
Q_SCALE = math.log2(math.e) / math.sqrt(D_HEAD)

VMEM_LIMIT_BYTES = 56 * 1024 * 1024
SUBLANES = 8
LANES = 128
BF16_ROWS = 16
ACC_ROWS = V_DIM + SUBLANES
HALO = 32


def _rms(x, g):
    return x * lax.rsqrt(jnp.mean(x * x, axis=-1, keepdims=True) + EPS) * g


def _silu(x):
    return x * jax.nn.sigmoid(x)


def _const_spec(shape):
    zeros = (0,) * len(shape)
    return pl.BlockSpec(shape, lambda *_: zeros, pipeline_mode=pl.Buffered(1))


def _lam_value(lq1, lk1, lq2, lk2, lam_init):
    a = jnp.sum(lq1 * lk1, axis=-1, keepdims=True)
    b = jnp.sum(lq2 * lk2, axis=-1, keepdims=True)
    return jnp.exp(a) - jnp.exp(b) + lam_init


def _ada_kernel(c_ref, w_ref, b_ref, o_ref):
    c = c_ref[...]
    a = _silu(c).astype(BF16)
    o_ref[...] = jnp.dot(a, w_ref[...].astype(BF16), preferred_element_type=F32) + b_ref[...]


def _ada(c_all, w_ada, b_ada):
    rows, d = c_all.shape
    n = w_ada.shape[1]
    tn = 1024 if n % 1024 == 0 else n
    return pl.pallas_call(
        _ada_kernel,
        out_shape=jax.ShapeDtypeStruct((rows, n), F32),
        grid=(n // tn,),
        in_specs=[
            pl.BlockSpec((rows, d), lambda j: (0, 0)),
            pl.BlockSpec((d, tn), lambda j: (0, j)),
            pl.BlockSpec((1, tn), lambda j: (0, j)),
        ],
        out_specs=pl.BlockSpec((rows, tn), lambda j: (0, j)),
        compiler_params=pltpu.CompilerParams(dimension_semantics=("arbitrary",)),
        name="ada_mod",
    )(c_all, w_ada, b_ada.reshape(1, n))


def _conv_ln_swish(ext_ref, sh_ref, y_ref, w_ref, cb_ref, lg_ref, lb_ref, yc_ref, *, tm, rc):
    n_sh = sh_ref.shape[1]
    for r in range(1, SUBLANES):
        sh_ref[r - 1] = ext_ref[r:r + n_sh, :]
    off = HALO - CONV_STATE
    for q in range(ext_ref.shape[1] // LANES):
        lanes = slice(q * LANES, (q + 1) * LANES)
        taps = [w_ref[j:j + 1, lanes] for j in range(CONV_K)]
        for c in range(tm // rc):
            acc = jnp.zeros((rc, LANES), F32)
            for j in range(CONV_K):
                a, r = divmod(off + j, SUBLANES)
                r0 = c * rc + a * SUBLANES
                win = ext_ref[r0:r0 + rc, lanes] if r == 0 else sh_ref[r - 1, r0:r0 + rc, lanes]
                acc = acc + taps[j] * win
            y_ref[c * rc:(c + 1) * rc, lanes] = acc
    for c in range(tm // rc):
        y = y_ref[c * rc:(c + 1) * rc, :] + cb_ref[...]
        mu = jnp.mean(y, axis=-1, keepdims=True)
        yc = y - mu
        yn = yc * lax.rsqrt(jnp.mean(yc * yc, axis=-1, keepdims=True) + EPS)
        yn = yn * lg_ref[...] + lb_ref[...]
        yc_ref[0, c * rc:(c + 1) * rc, :] = _silu(yn).astype(yc_ref.dtype)


def _front_prompt_kernel(x_ref, mods_ref, g_ref, wglu_ref, wkv_ref, wqvt_ref, bglu_ref,
                         cw_ref, cb_ref, lg_ref, lb_ref,
                         k_ref, v_ref, kbf_ref, qt_ref, vt_ref, yc_ref, tail_ref,
                         ext_ref, sh_ref, y_ref, *, d, aw, cw, tm, rc):
    si = pl.program_id(1)
    x = x_ref[0]
    shift = mods_ref[0, :, 0:d]
    scale = mods_ref[0, :, d:2 * d]
    h = _rms(x, g_ref[...]) * (1.0 + scale) + shift
    hb = h.astype(BF16)

    ga = jnp.dot(hb, wglu_ref[...], preferred_element_type=F32) + bglu_ref[...]
    u = ga[:, 0:cw] * jax.nn.sigmoid(ga[:, cw:])

    @pl.when(si == 0)
    def _():
        ext_ref[0:HALO, :] = jnp.zeros((HALO, cw), F32)

    @pl.when(si > 0)
    def _():
        ext_ref[0:HALO, :] = ext_ref[tm:tm + HALO, :]

    ext_ref[HALO:HALO + tm, :] = u
    tail_ref[0] = u[tm - HALO:tm, :]
    _conv_ln_swish(ext_ref, sh_ref, y_ref, cw_ref, cb_ref, lg_ref, lb_ref, yc_ref, tm=tm, rc=rc)

    z = jnp.dot(hb, wkv_ref[...], preferred_element_type=F32)
    k = z[:, 0:aw]
    v = z[:, aw:2 * aw]
    for hd in range(N_HEADS):
        k_ref[0, pl.ds(hd, tm, stride=N_HEADS), :] = k[:, hd * QK_DIM:(hd + 1) * QK_DIM]
        v_ref[0, pl.ds(hd, tm, stride=N_HEADS), :] = v[:, hd * V_DIM:(hd + 1) * V_DIM]
    kbf_ref[0] = k.astype(BF16)
    zt = lax.dot_general(wqvt_ref[...], hb, (((1,), (1,)), ((), ())),
                         preferred_element_type=F32)
    qt_ref[0] = (zt[0:aw] * Q_SCALE).astype(BF16)
    vt_ref[0] = zt[aw:].astype(BF16)


def _front_prompt(x, mods, g_pre, w_glu, w_kv, w_qvt, b_glu, conv_w, conv_b, ln_g, ln_b, *, tm):
    b, s, d = x.shape
    aw = w_qvt.shape[0] // 2
    cw = b_glu.shape[-1] // 2
    assert tm >= HALO and tm % SUBLANES == 0
    rc = min(128, tm)
    kern = functools.partial(_front_prompt_kernel, d=d, aw=aw, cw=cw, tm=tm, rc=rc)
    row_blk = lambda w: pl.BlockSpec((1, tm, w), lambda i, j: (i, j, 0))
    head_blk = lambda w: pl.BlockSpec((1, tm * N_HEADS, w), lambda i, j: (i, j, 0))
    col_blk = pl.BlockSpec((1, aw, tm), lambda i, j: (i, 0, j))
    return pl.pallas_call(
        kern,
        out_shape=(
            jax.ShapeDtypeStruct((b, s * N_HEADS, QK_DIM), F32),
            jax.ShapeDtypeStruct((b, s * N_HEADS, V_DIM), F32),
            jax.ShapeDtypeStruct((b, s, aw), BF16),
            jax.ShapeDtypeStruct((b, aw, s), BF16),
            jax.ShapeDtypeStruct((b, aw, s), BF16),
            jax.ShapeDtypeStruct((b, s, cw), BF16),
            jax.ShapeDtypeStruct((b, HALO, cw), F32),
        ),
        grid=(b, s // tm),
        in_specs=[
            row_blk(d),
            pl.BlockSpec((1, 1, N_ADA * d), lambda i, j: (i, 0, 0)),
            _const_spec((1, d)),
            _const_spec(w_glu.shape),
            _const_spec(w_kv.shape),
            _const_spec(w_qvt.shape),
            _const_spec((1, 2 * cw)),
            _const_spec((CONV_K, cw)),
            _const_spec((1, cw)),
            _const_spec((1, cw)),
            _const_spec((1, cw)),
        ],
        out_specs=(head_blk(QK_DIM), head_blk(V_DIM), row_blk(aw), col_blk, col_blk, row_blk(cw),
                   pl.BlockSpec((1, HALO, cw), lambda i, j: (i, 0, 0))),
        scratch_shapes=[pltpu.VMEM((HALO + tm, cw), F32),
                        pltpu.VMEM((SUBLANES - 1, HALO + tm - SUBLANES, cw), F32),
                        pltpu.VMEM((tm, cw), F32)],
        compiler_params=pltpu.CompilerParams(
            dimension_semantics=("arbitrary", "arbitrary"),
            vmem_limit_bytes=VMEM_LIMIT_BYTES),
        name="front_prompt",
    )(x, mods, g_pre, w_glu, w_kv, w_qvt, b_glu, conv_w, conv_b, ln_g, ln_b)


def _attn_prompt_kernel(qt_ref, k_ref, vt_ref, lq1_ref, lk1_ref, lq2_ref, lk2_ref, g_ref,
                        o_ref, w_ref, sa_ref, sb_ref, acc_ref, *, tq, tk, lam_init):
    per_q = tq // tk
    qi = pl.program_id(2)
    qt = qt_ref[0]
    row = lax.broadcasted_iota(jnp.int32, qt.shape, 0)
    zero = jnp.zeros_like(qt)
    w_ref[0] = jnp.where(row < D_HEAD, qt, zero)
    w_ref[1] = jnp.where(row >= D_HEAD, qt, zero)
    acc_ref[...] = jnp.zeros_like(acc_ref)
    ones_rows = (lax.broadcasted_iota(jnp.int32, (BF16_ROWS, tk), 0) == 0).astype(BF16)

    def key_slice(chunk):
        start = chunk * tk
        return pl.ds(start if isinstance(start, int) else pl.multiple_of(start, tk), tk)

    def scores(chunk, s_ref, lo=0):
        kblk = k_ref[0, key_slice(chunk), :]
        for c in range(2):
            s_ref[c, :, lo:] = jnp.dot(kblk, w_ref[c, :, lo:], preferred_element_type=F32)

    def softmax_pv(chunk, s_ref, carry, mask, lo=0):
        vblk = jnp.concatenate([vt_ref[0, :, key_slice(chunk)], ones_rows], axis=0)
        out = []
        for c in range(2):
            m = carry[c][:, lo:]
            s = s_ref[c, :, lo:]
            if mask is not None:
                s = jnp.where(mask[:, lo:], s, NEG)
            m_new = jnp.maximum(m, jnp.max(s, axis=0, keepdims=True))
            alpha = jnp.exp2(m - m_new)
            p = jnp.exp2((s - m_new).astype(BF16))
            pv = jnp.dot(vblk, p, preferred_element_type=F32)
            acc_ref[c, :, lo:] = alpha * acc_ref[c, :, lo:] + pv[0:ACC_ROWS]
            out.append(m_new if lo == 0 else jnp.concatenate([carry[c][:, :lo], m_new], axis=1))
        return tuple(out)

    init = jnp.full((1, tq), NEG, F32)
    scores(0, sa_ref)

    def pair(j, carry):
        scores(2 * j + 1, sb_ref)
        carry = softmax_pv(2 * j, sa_ref, carry, None)
        scores(2 * j + 2, sa_ref)
        return softmax_pv(2 * j + 1, sb_ref, carry, None)

    carry = lax.fori_loop(0, qi * (per_q // 2), pair, (init, init))
    key_rel = lax.broadcasted_iota(jnp.int32, (tk, tq), 0)
    q_rel = lax.broadcasted_iota(jnp.int32, (tk, tq), 1)
    base = qi * per_q
    bufs = (sa_ref, sb_ref)
    for c in range(per_q):
        if c + 1 < per_q:
            scores(base + c + 1, bufs[(c + 1) % 2], lo=(c + 1) * tk)
        carry = softmax_pv(base + c, bufs[c % 2], carry, key_rel + c * tk <= q_rel, lo=c * tk)

    lam = _lam_value(lq1_ref[...], lk1_ref[...], lq2_ref[...], lk2_ref[...], lam_init)
    l1 = acc_ref[0, V_DIM:V_DIM + 1, :]
    l2 = acc_ref[1, V_DIM:V_DIM + 1, :]
    ot = acc_ref[0, 0:V_DIM, :] * (1.0 / l1) - lam * (acc_ref[1, 0:V_DIM, :] * (1.0 / l2))
    ot = ot * lax.rsqrt(jnp.mean(ot * ot, axis=0, keepdims=True) + EPS)
    o = ot.T * g_ref[...] * (1.0 - lam_init)
    o_ref[0] = o.astype(o_ref.dtype)


def _attn_prompt(qt, kbf, vt, lam_rows, subln_g, *, tq, tk, lam_init):
    b, aw, s = qt.shape
    assert tq % (2 * tk) == 0
    kern = functools.partial(_attn_prompt_kernel, tq=tq, tk=tk, lam_init=lam_init)
    small = _const_spec((1, D_HEAD))
    return pl.pallas_call(
        kern,
        out_shape=jax.ShapeDtypeStruct((b, s, aw), BF16),
        grid=(b, N_HEADS, s // tq),
        in_specs=[
            pl.BlockSpec((1, QK_DIM, tq), lambda i, h, j: (i, h, j)),
            pl.BlockSpec((1, s, QK_DIM), lambda i, h, j: (i, 0, h)),
            pl.BlockSpec((1, V_DIM, s), lambda i, h, j: (i, h, 0)),
            small, small, small, small,
            _const_spec((1, V_DIM)),
        ],
        out_specs=pl.BlockSpec((1, tq, V_DIM), lambda i, h, j: (i, j, h)),
        scratch_shapes=[pltpu.VMEM((2, QK_DIM, tq), BF16),
                        pltpu.VMEM((2, tk, tq), F32), pltpu.VMEM((2, tk, tq), F32),
                        pltpu.VMEM((2, ACC_ROWS, tq), F32)],
        compiler_params=pltpu.CompilerParams(
            dimension_semantics=("arbitrary", "arbitrary", "arbitrary"),
            vmem_limit_bytes=VMEM_LIMIT_BYTES),
        name="attn_prompt",
    )(qt, kbf, vt, *lam_rows, subln_g)


def _post_kernel(ya_ref, yc_ref, x_ref, mods_ref, gpm_ref, gpf_ref, gpo_ref,
                 wout_ref, wg_ref, wu_ref, wd_ref, y_ref, acc_ref, *, d, aw, ff_chunks):
    gate1 = mods_ref[0, :, 2 * d:3 * d]
    shift2 = mods_ref[0, :, 3 * d:4 * d]
    scale2 = mods_ref[0, :, 4 * d:5 * d]
    gate2 = mods_ref[0, :, 5 * d:6 * d]
    m = jnp.dot(ya_ref[0].astype(BF16), wout_ref[0:aw, :], preferred_element_type=F32)
    m = m + jnp.dot(yc_ref[0].astype(BF16), wout_ref[aw:, :], preferred_element_type=F32)
    x1 = x_ref[0] + gate1 * _rms(m, gpm_ref[...])
    hb = (_rms(x1, gpf_ref[...]) * (1.0 + scale2) + shift2).astype(BF16)
    for n, (c0, c1) in enumerate(ff_chunks):
        g = jnp.dot(hb, wg_ref[:, c0:c1], preferred_element_type=F32)
        u = jnp.dot(hb, wu_ref[:, c0:c1], preferred_element_type=F32)
        a = (_silu(g) * u).astype(BF16)
        part = jnp.dot(a, wd_ref[c0:c1, :], preferred_element_type=F32)
        if n == 0:
            acc_ref[...] = part
        else:
            acc_ref[...] += part
    y_ref[0] = x1 + gate2 * _rms(acc_ref[...], gpo_ref[...])


def _post(ya, yc, x, mods, g_post_mix, g_pre_ffn, g_post_ffn, w_out, w_gate, w_up, w_down, *, tm):
    b, s, d = x.shape
    aw = ya.shape[-1]
    cw = yc.shape[-1]
    dff = w_gate.shape[1]
    step = 512
    ff_chunks = tuple((c, min(c + step, dff)) for c in range(0, dff, step))
    mod_rows = mods.shape[1]
    mt = 1 if mod_rows == 1 else tm
    kern = functools.partial(_post_kernel, d=d, aw=aw, ff_chunks=ff_chunks)
    return pl.pallas_call(
        kern,
        out_shape=jax.ShapeDtypeStruct((b, s, d), F32),
        grid=(b, s // tm),
        in_specs=[
            pl.BlockSpec((1, tm, aw), lambda i, j: (i, j, 0)),
            pl.BlockSpec((1, tm, cw), lambda i, j: (i, j, 0)),
            pl.BlockSpec((1, tm, d), lambda i, j: (i, j, 0)),
            pl.BlockSpec((1, mt, N_ADA * d),
                         (lambda i, j: (i, 0, 0)) if mod_rows == 1 else (lambda i, j: (i, j, 0))),
            _const_spec((1, d)), _const_spec((1, d)), _const_spec((1, d)),
            _const_spec(w_out.shape), _const_spec(w_gate.shape),
            _const_spec(w_up.shape), _const_spec(w_down.shape),
        ],
        out_specs=pl.BlockSpec((1, tm, d), lambda i, j: (i, j, 0)),
        scratch_shapes=[pltpu.VMEM((tm, d), F32)],
        compiler_params=pltpu.CompilerParams(
            dimension_semantics=("arbitrary", "arbitrary"),
            vmem_limit_bytes=VMEM_LIMIT_BYTES),
        name="post_ffn",
    )(ya, yc, x, mods, g_post_mix, g_pre_ffn, g_post_ffn, w_out, w_gate, w_up, w_down)


def _front_sample_kernel(x_ref, mods_ref, g_ref, win_ref, bglu_ref, st_ref, cw_ref, cb_ref,
                         lg_ref, lb_ref, q_ref, k_ref, v_ref, u_ref, yc_ref, *, d, aw, cw):
    x = x_ref[...]
    shift = mods_ref[:, 0:d]
    scale = mods_ref[:, d:2 * d]
    hb = (_rms(x, g_ref[...]) * (1.0 + scale) + shift).astype(BF16)
    z = jnp.dot(hb, win_ref[...], preferred_element_type=F32)
    q_ref[...] = z[:, 0:aw] * Q_SCALE
    k_ref[...] = z[:, aw:2 * aw]
    v_ref[...] = z[:, 2 * aw:3 * aw]
    ga = z[:, 3 * aw:] + bglu_ref[...]
    u = ga[:, 0:cw] * jax.nn.sigmoid(ga[:, cw:])
    u_ref[...] = u
    y = cw_ref[CONV_STATE:CONV_K, :] * u + cb_ref[...]
    for j in range(CONV_STATE):
        y = y + cw_ref[j:j + 1, :] * st_ref[j]
    mu = jnp.mean(y, axis=-1, keepdims=True)
    yc = y - mu
    yn = yc * lax.rsqrt(jnp.mean(yc * yc, axis=-1, keepdims=True) + EPS)
    yc_ref[...] = _silu(yn * lg_ref[...] + lb_ref[...]).astype(yc_ref.dtype)


def _front_sample(x, mods, g_pre, w_in, b_glu, state_t, conv_w, conv_b, ln_g, ln_b):
    rows, d = x.shape
    cw = conv_w.shape[1]
    aw = (w_in.shape[1] - 2 * cw) // 3
    kern = functools.partial(_front_sample_kernel, d=d, aw=aw, cw=cw)
    full = lambda a: pl.BlockSpec(a.shape, lambda i: (0,) * a.ndim)
    args = (x, mods, g_pre, w_in, b_glu, state_t, conv_w, conv_b, ln_g, ln_b)
    out_shapes = (
        jax.ShapeDtypeStruct((rows, aw), F32),
        jax.ShapeDtypeStruct((rows, aw), F32),
        jax.ShapeDtypeStruct((rows, aw), F32),
        jax.ShapeDtypeStruct((rows, cw), F32),
        jax.ShapeDtypeStruct((rows, cw), BF16),
    )
    return pl.pallas_call(
        kern,
        out_shape=out_shapes,
        grid=(1,),
        in_specs=[full(a) for a in args],
        out_specs=tuple(pl.BlockSpec(o.shape, lambda i: (0, 0)) for o in out_shapes),
        compiler_params=pltpu.CompilerParams(
            dimension_semantics=("arbitrary",), vmem_limit_bytes=VMEM_LIMIT_BYTES),
        name="front_sample",
    )(*args)


ROWS = 16
OUT_ROWS = 8
GROUPS = 2
N_SLOTS = 3


def _head_rows(x_row, sub, lane):
    seg_of_lane = lax.shift_right_logical(lane, 7)
    full = jnp.where(seg_of_lane == jnp.bitwise_and(sub, N_HEADS - 1), x_row, 0.0)
    out = full[:, 0:128]
    for h in range(1, N_HEADS):
        out = out + full[:, h * 128:(h + 1) * 128]
    return out


def _attn_sample_kernel(pt_ref, q_ref, kn_ref, vn_ref, lq1_ref, lk1_ref, lq2_ref, lk2_ref, g_ref,
                        ck_ref, cv_ref, o_ref, kbuf_ref, vbuf_ref, sem_ref, m_ref, l_ref, acc_ref,
                        *, pp, n_steps, total_steps, lam_init):
    b = pl.program_id(0)
    j = pl.program_id(1)
    t = b * n_steps + j

    def page_copies(step, slot):
        seq = step // n_steps
        first = (step - seq * n_steps) * pp
        copies = []
        for i in range(pp):
            page = pt_ref[seq, first + i]
            copies.append(pltpu.make_async_copy(ck_ref.at[page], kbuf_ref.at[slot, i], sem_ref.at[0, slot]))
            copies.append(pltpu.make_async_copy(cv_ref.at[page], vbuf_ref.at[slot, i], sem_ref.at[1, slot]))
        return copies

    @pl.when(t == 0)
    def _():
        for s in range(N_SLOTS - 1):
            for c in page_copies(s, s):
                c.start()

    ahead = t + (N_SLOTS - 1)

    @pl.when(ahead < total_steps)
    def _():
        for c in page_copies(ahead, lax.rem(ahead, N_SLOTS)):
            c.start()

    slot = lax.rem(t, N_SLOTS)
    for c in page_copies(t, slot):
        c.wait()

    aw = q_ref.shape[-1]
    sub_w = lax.broadcasted_iota(jnp.int32, (ROWS, aw), 0)
    lane_w = lax.broadcasted_iota(jnp.int32, (ROWS, aw), 1)
    sub = lax.broadcasted_iota(jnp.int32, (ROWS, QK_DIM), 0)
    lane = lax.broadcasted_iota(jnp.int32, (ROWS, QK_DIM), 1)
    own_map = lax.shift_right_logical(lane, 6) == lax.shift_right_logical(sub, 2)
    qq = jnp.where(own_map, _head_rows(q_ref[0], sub_w, lane_w), 0.0)
    qq_bf = qq.astype(BF16)

    @pl.when(j == 0)
    def _():
        m_ref[...] = jnp.full(m_ref.shape, NEG, F32)
        l_ref[...] = jnp.zeros(l_ref.shape, F32)
        acc_ref[...] = jnp.zeros(acc_ref.shape, F32)

    n_rows = kbuf_ref.shape[2]
    col = lax.broadcasted_iota(jnp.int32, (ROWS, n_rows), 1)
    srow = lax.broadcasted_iota(jnp.int32, (ROWS, n_rows), 0)
    same_head = jnp.bitwise_and(col, N_HEADS - 1) == jnp.bitwise_and(srow, N_HEADS - 1)
    per_group = pp // GROUPS
    for g in range(GROUPS):
        pages = range(g * per_group, (g + 1) * per_group)
        parts = []
        for i in pages:
            s_i = lax.dot_general(qq_bf, kbuf_ref[slot, i].astype(BF16), (((1,), (1,)), ((), ())),
                                  preferred_element_type=F32)
            parts.append(jnp.where(same_head, s_i, NEG))
        s = jnp.concatenate(parts, axis=1)
        m_old = m_ref[g]
        m_new = jnp.maximum(m_old, jnp.max(s, axis=1, keepdims=True))
        alpha = jnp.exp2(m_old - m_new)
        p = jnp.exp2(s - m_new)
        l_ref[g] = alpha * l_ref[g] + jnp.sum(p, axis=1, keepdims=True)
        m_ref[g] = m_new
        pv = jnp.zeros((ROWS, V_DIM), F32)
        for n, i in enumerate(pages):
            pv = pv + jnp.dot(p[:, n * n_rows:(n + 1) * n_rows].astype(BF16),
                              vbuf_ref[slot, i].astype(BF16), preferred_element_type=F32)
        acc_ref[g] = alpha * acc_ref[g] + pv

    @pl.when(j == n_steps - 1)
    def _():
        k16 = _head_rows(kn_ref[0], sub_w, lane_w)
        v16 = _head_rows(vn_ref[0], sub_w, lane_w)
        s_new = jnp.sum(qq * k16, axis=1, keepdims=True)
        m_fin = s_new
        for g in range(GROUPS):
            m_fin = jnp.maximum(m_fin, m_ref[g])
        p_new = jnp.exp2(s_new - m_fin)
        l_fin = p_new
        acc = p_new * v16
        for g in range(GROUPS):
            alpha = jnp.exp2(m_ref[g] - m_fin)
            l_fin = l_fin + alpha * l_ref[g]
            acc = acc + alpha * acc_ref[g]
        o = acc * (1.0 / l_fin)
        lam = _lam_value(lq1_ref[...], lk1_ref[...], lq2_ref[...], lk2_ref[...], lam_init)
        oc = o - lam * pltpu.roll(o, ROWS - N_HEADS, axis=0)
        y = oc * lax.rsqrt(jnp.mean(oc * oc, axis=-1, keepdims=True) + EPS)
        y = y * g_ref[...] * (1.0 - lam_init)
        o_ref[0] = y[0:OUT_ROWS]


def _attn_sample(page_table, q, k_new, v_new, lam_rows, subln_g, cache_k, cache_v, *, pp, lam_init):
    db, n_pages = page_table.shape
    n_rows = cache_k.shape[1]
    aw = q.shape[-1]
    n_steps = n_pages // pp
    total_steps = db * n_steps
    assert pp % GROUPS == 0 and total_steps >= N_SLOTS
    kern = functools.partial(_attn_sample_kernel, pp=pp, n_steps=n_steps, total_steps=total_steps,
                             lam_init=lam_init)
    row = pl.BlockSpec((1, 1, aw), lambda b, j, pt: (b, 0, 0))
    small = pl.BlockSpec((1, D_HEAD), lambda b, j, pt: (0, 0))
    hbm = pl.BlockSpec(memory_space=pl.ANY)

    grid_spec = pltpu.PrefetchScalarGridSpec(
        num_scalar_prefetch=1,
        grid=(db, n_steps),
        in_specs=[row, row, row, small, small, small, small,
                  pl.BlockSpec((1, V_DIM), lambda b, j, pt: (0, 0)), hbm, hbm],
        out_specs=pl.BlockSpec((1, OUT_ROWS, V_DIM), lambda b, j, pt: (b, 0, 0)),
        scratch_shapes=[pltpu.VMEM((N_SLOTS, pp, n_rows, QK_DIM), F32),
                        pltpu.VMEM((N_SLOTS, pp, n_rows, V_DIM), F32),
                        pltpu.SemaphoreType.DMA((2, N_SLOTS)),
                        pltpu.VMEM((GROUPS, ROWS, 1), F32), pltpu.VMEM((GROUPS, ROWS, 1), F32),
                        pltpu.VMEM((GROUPS, ROWS, V_DIM), F32)],
    )
    out = pl.pallas_call(
        kern,
        out_shape=jax.ShapeDtypeStruct((db, OUT_ROWS, V_DIM), F32),
        grid_spec=grid_spec,
        compiler_params=pltpu.CompilerParams(
            dimension_semantics=("arbitrary", "arbitrary"),
            vmem_limit_bytes=VMEM_LIMIT_BYTES),
        name="attn_sample",
    )(page_table, q.reshape(db, 1, aw), k_new.reshape(db, 1, aw), v_new.reshape(db, 1, aw),
      *lam_rows, subln_g, cache_k, cache_v)
    return out[:, 0:N_HEADS, :].reshape(db, N_HEADS * V_DIM)


def _tile(n, pref):
    t = min(n, pref)
    while n % t:
        t //= 2
    return t


def kernel(x_prompt, x_sample, c_prompt, c_sample, cache_k, cache_v, state_conv, page_table, w_ada, b_ada, g_pre_mix, g_post_mix, g_pre_ffn, g_post_ffn, w_in, b_glu, conv_w, conv_b, conv_ln_g, conv_ln_b, lambda_q1, lambda_k1, lambda_q2, lambda_k2, subln_g, w_out, w_gate, w_up, w_down):
    depth = w_in.shape[0]
    assert depth == 1, "single-layer step"
    l = 0
    lam_init = 0.8 - 0.6 * math.exp(-0.3 * l)
    b, s, d = x_prompt.shape
    db = x_sample.shape[0]
    assert x_sample.shape[1] == 1
    cw = conv_w.shape[-1]
    aw = N_HEADS * V_DIM
    n_pool, page = cache_k.shape[1], cache_k.shape[2]

    row = lambda a: a[l].reshape(1, -1)
    lam_rows = (row(lambda_q1), row(lambda_k1), row(lambda_q2), row(lambda_k2))
    g_sub = row(subln_g)

    mods = _ada(jnp.concatenate([c_prompt, c_sample], axis=0), w_ada[l], b_ada[l])
    mods_p = mods[:b].reshape(b, 1, N_ADA * d)
    mods_s = mods[b:]

    w_in_bf = w_in[l].astype(BF16)
    w_kv = w_in_bf[:, aw:3 * aw]
    w_glu = w_in_bf[:, 3 * aw:]
    w_qvt = jnp.concatenate([w_in_bf[:, 0:aw], w_in_bf[:, 2 * aw:3 * aw]], axis=1).T
    w_out_bf = w_out[l].astype(BF16)
    w_gate_bf = w_gate[l].astype(BF16)
    w_up_bf = w_up[l].astype(BF16)
    w_down_bf = w_down[l].astype(BF16)

    state = state_conv[l]
    q_s, k_s, v_s, u_s, yc_s = _front_sample(
        x_sample.reshape(db, d), mods_s, row(g_pre_mix), w_in_bf, row(b_glu),
        jnp.transpose(state, (1, 0, 2)), conv_w[l], row(conv_b), row(conv_ln_g), row(conv_ln_b))
    pp = _tile(page_table.shape[1], 16)
    ya_s = _attn_sample(page_table + l * n_pool, q_s, k_s, v_s, lam_rows, g_sub,
                        cache_k.reshape(depth * n_pool, page * N_HEADS, QK_DIM),
                        cache_v.reshape(depth * n_pool, page * N_HEADS, V_DIM),
                        pp=pp, lam_init=lam_init)
    y_sample = _post(ya_s.reshape(1, db, aw), yc_s.reshape(1, db, cw), x_sample.reshape(1, db, d),
                     mods_s.reshape(1, db, N_ADA * d), row(g_post_mix), row(g_pre_ffn),
                     row(g_post_ffn), w_out_bf, w_gate_bf, w_up_bf, w_down_bf, tm=_tile(db, 512))
    conv_s = jnp.concatenate([state[:, 1:, :], u_s[:, None, :]], axis=1)

    tm = _tile(s, 512)
    k_p, v_p, kbf, qt, vt, yc_p, u_tail = _front_prompt(
        x_prompt, mods_p, row(g_pre_mix), w_glu, w_kv, w_qvt, row(b_glu),
        conv_w[l], row(conv_b), row(conv_ln_g), row(conv_ln_b), tm=tm)
    tq = _tile(s, 2048)
    ya_p = _attn_prompt(qt, kbf, vt, lam_rows, g_sub, tq=tq, tk=tq // 8, lam_init=lam_init)
    y_prompt = _post(ya_p, yc_p, x_prompt, mods_p, row(g_post_mix), row(g_pre_ffn), row(g_post_ffn),
                     w_out_bf, w_gate_bf, w_up_bf, w_down_bf, tm=tm)
    conv_p = u_tail[:, HALO - CONV_STATE:, :]

    return (y_prompt,
            y_sample.reshape(db, 1, d),
            k_p.reshape(1, b, s, N_HEADS, QK_DIM),
            v_p.reshape(1, b, s, N_HEADS, V_DIM),
            conv_p[None],
            k_s.reshape(1, db, 1, N_HEADS, QK_DIM),
            v_s.reshape(1, db, 1, N_HEADS, V_DIM),
            conv_s[None])
```

```python
import functools
import math

import jax
import jax.numpy as jnp
from jax import lax
from jax.experimental import pallas as pl
from jax.experimental.pallas import tpu as pltpu

F32 = jnp.float32
BF16 = jnp.bfloat16

N_HEADS = 4
D_HEAD = 64
QK_DIM = 2 * D_HEAD
V_DIM = 128
CONV_K = 31
CONV_STATE = CONV_K - 1
N_ADA = 6
EPS = 1e-6
NEG = -1e30
Q_SCALE = math.log2(math.e) / math.sqrt(D_HEAD)

VMEM_LIMIT_BYTES = 56 * 1024 * 1024
SUBLANES = 8
LANES = 128
BF16_ROWS = 16
ACC_ROWS = V_DIM + SUBLANES
HALO = 32


def _rms(x, g):
    return x * lax.rsqrt(jnp.mean(x * x, axis=-1, keepdims=True) + EPS) * g


def _silu(x):
    return x * jax.nn.sigmoid(x)


def _const_spec(shape):
    zeros = (0,) * len(shape)
    return pl.BlockSpec(shape, lambda *_: zeros, pipeline_mode=pl.Buffered(1))


def _lam_value(lq1, lk1, lq2, lk2, lam_init):
    a = jnp.sum(lq1 * lk1, axis=-1, keepdims=True)
    b = jnp.sum(lq2 * lk2, axis=-1, keepdims=True)
    return jnp.exp(a) - jnp.exp(b) + lam_init


def _ada_kernel(c_ref, w_ref, b_ref, o_ref):
    c = c_ref[...]
    a = _silu(c).astype(BF16)
    o_ref[...] = jnp.dot(a, w_ref[...].astype(BF16), preferred_element_type=F32) + b_ref[...]


def _ada(c_all, w_ada, b_ada):
    rows, d = c_all.shape
    n = w_ada.shape[1]
    tn = 1024 if n % 1024 == 0 else n
    return pl.pallas_call(
        _ada_kernel,
        out_shape=jax.ShapeDtypeStruct((rows, n), F32),
        grid=(n // tn,),
        in_specs=[
            pl.BlockSpec((rows, d), lambda j: (0, 0)),
            pl.BlockSpec((d, tn), lambda j: (0, j)),
            pl.BlockSpec((1, tn), lambda j: (0, j)),
        ],
        out_specs=pl.BlockSpec((rows, tn), lambda j: (0, j)),
        compiler_params=pltpu.CompilerParams(dimension_semantics=("arbitrary",)),
        name="ada_mod",
    )(c_all, w_ada, b_ada.reshape(1, n))


def _conv_ln_swish(ext_ref, sh_ref, y_ref, w_ref, cb_ref, lg_ref, lb_ref, yc_ref, *, tm, rc):
    n_sh = sh_ref.shape[1]
    for r in range(1, SUBLANES):
        sh_ref[r - 1] = ext_ref[r:r + n_sh, :]
    off = HALO - CONV_STATE
    for q in range(ext_ref.shape[1] // LANES):
        lanes = slice(q * LANES, (q + 1) * LANES)
        taps = [w_ref[j:j + 1, lanes] for j in range(CONV_K)]
        for c in range(tm // rc):
            acc = jnp.zeros((rc, LANES), F32)
            for j in range(CONV_K):
                a, r = divmod(off + j, SUBLANES)
                r0 = c * rc + a * SUBLANES
                win = ext_ref[r0:r0 + rc, lanes] if r == 0 else sh_ref[r - 1, r0:r0 + rc, lanes]
                acc = acc + taps[j] * win
            y_ref[c * rc:(c + 1) * rc, lanes] = acc
    for c in range(tm // rc):
        y = y_ref[c * rc:(c + 1) * rc, :] + cb_ref[...]
        mu = jnp.mean(y, axis=-1, keepdims=True)
        yc = y - mu
        yn = yc * lax.rsqrt(jnp.mean(yc * yc, axis=-1, keepdims=True) + EPS)
        yn = yn * lg_ref[...] + lb_ref[...]
        yc_ref[0, c * rc:(c + 1) * rc, :] = _silu(yn).astype(yc_ref.dtype)


def _front_prompt_kernel(x_ref, mods_ref, g_ref, wglu_ref, wkv_ref, wqvt_ref, bglu_ref,
                         cw_ref, cb_ref, lg_ref, lb_ref,
                         k_ref, v_ref, kbf_ref, qt_ref, vt_ref, yc_ref, tail_ref,
                         ext_ref, sh_ref, y_ref, *, d, aw, cw, tm, rc):
    si = pl.program_id(1)
    x = x_ref[0]
    shift = mods_ref[0, :, 0:d]
    scale = mods_ref[0, :, d:2 * d]
    h = _rms(x, g_ref[...]) * (1.0 + scale) + shift
    hb = h.astype(BF16)

    ga = jnp.dot(hb, wglu_ref[...], preferred_element_type=F32) + bglu_ref[...]
    u = ga[:, 0:cw] * jax.nn.sigmoid(ga[:, cw:])

    @pl.when(si == 0)
    def _():
        ext_ref[0:HALO, :] = jnp.zeros((HALO, cw), F32)

    @pl.when(si > 0)
    def _():
        ext_ref[0:HALO, :] = ext_ref[tm:tm + HALO, :]

    ext_ref[HALO:HALO + tm, :] = u
    tail_ref[0] = u[tm - HALO:tm, :]
    _conv_ln_swish(ext_ref, sh_ref, y_ref, cw_ref, cb_ref, lg_ref, lb_ref, yc_ref, tm=tm, rc=rc)

    z = jnp.dot(hb, wkv_ref[...], preferred_element_type=F32)
    k = z[:, 0:aw]
    v = z[:, aw:2 * aw]
    for hd in range(N_HEADS):
        k_ref[0, pl.ds(hd, tm, stride=N_HEADS), :] = k[:, hd * QK_DIM:(hd + 1) * QK_DIM]
        v_ref[0, pl.ds(hd, tm, stride=N_HEADS), :] = v[:, hd * V_DIM:(hd + 1) * V_DIM]
    kbf_ref[0] = k.astype(BF16)
    zt = lax.dot_general(wqvt_ref[...], hb, (((1,), (1,)), ((), ())),
                         preferred_element_type=F32)
    qt_ref[0] = (zt[0:aw] * Q_SCALE).astype(BF16)
    vt_ref[0] = zt[aw:].astype(BF16)


def _front_prompt(x, mods, g_pre, w_glu, w_kv, w_qvt, b_glu, conv_w, conv_b, ln_g, ln_b, *, tm):
    b, s, d = x.shape
    aw = w_qvt.shape[0] // 2
    cw = b_glu.shape[-1] // 2
    assert tm >= HALO and tm % SUBLANES == 0
    rc = min(128, tm)
    kern = functools.partial(_front_prompt_kernel, d=d, aw=aw, cw=cw, tm=tm, rc=rc)
    row_blk = lambda w: pl.BlockSpec((1, tm, w), lambda i, j: (i, j, 0))
    head_blk = lambda w: pl.BlockSpec((1, tm * N_HEADS, w), lambda i, j: (i, j, 0))
    col_blk = pl.BlockSpec((1, aw, tm), lambda i, j: (i, 0, j))
    return pl.pallas_call(
        kern,
        out_shape=(
            jax.ShapeDtypeStruct((b, s * N_HEADS, QK_DIM), F32),
            jax.ShapeDtypeStruct((b, s * N_HEADS, V_DIM), F32),
            jax.ShapeDtypeStruct((b, s, aw), BF16),
            jax.ShapeDtypeStruct((b, aw, s), BF16),
            jax.ShapeDtypeStruct((b, aw, s), BF16),
            jax.ShapeDtypeStruct((b, s, cw), BF16),
            jax.ShapeDtypeStruct((b, HALO, cw), F32),
        ),
        grid=(b, s // tm),
        in_specs=[
            row_blk(d),
            pl.BlockSpec((1, 1, N_ADA * d), lambda i, j: (i, 0, 0)),
            _const_spec((1, d)),
            _const_spec(w_glu.shape),
            _const_spec(w_kv.shape),
            _const_spec(w_qvt.shape),
            _const_spec((1, 2 * cw)),
            _const_spec((CONV_K, cw)),
            _const_spec((1, cw)),
            _const_spec((1, cw)),
            _const_spec((1, cw)),
        ],
        out_specs=(head_blk(QK_DIM), head_blk(V_DIM), row_blk(aw), col_blk, col_blk, row_blk(cw),
                   pl.BlockSpec((1, HALO, cw), lambda i, j: (i, 0, 0))),
        scratch_shapes=[pltpu.VMEM((HALO + tm, cw), F32),
                        pltpu.VMEM((SUBLANES - 1, HALO + tm - SUBLANES, cw), F32),
                        pltpu.VMEM((tm, cw), F32)],
        compiler_params=pltpu.CompilerParams(
            dimension_semantics=("arbitrary", "arbitrary"),
            vmem_limit_bytes=VMEM_LIMIT_BYTES),
        name="front_prompt",
    )(x, mods, g_pre, w_glu, w_kv, w_qvt, b_glu, conv_w, conv_b, ln_g, ln_b)


def _attn_prompt_kernel(qt_ref, k_ref, vt_ref, lq1_ref, lk1_ref, lq2_ref, lk2_ref, g_ref,
                        o_ref, w_ref, sa_ref, sb_ref, acc_ref, *, tq, tk, lam_init):
    per_q = tq // tk
    qi = pl.program_id(2)
    qt = qt_ref[0]
    row = lax.broadcasted_iota(jnp.int32, qt.shape, 0)
    zero = jnp.zeros_like(qt)
    w_ref[0] = jnp.where(row < D_HEAD, qt, zero)
    w_ref[1] = jnp.where(row >= D_HEAD, qt, zero)
    acc_ref[...] = jnp.zeros_like(acc_ref)
    ones_rows = (lax.broadcasted_iota(jnp.int32, (BF16_ROWS, tk), 0) == 0).astype(BF16)

    def key_slice(chunk):
        start = chunk * tk
        return pl.ds(start if isinstance(start, int) else pl.multiple_of(start, tk), tk)

    def scores(chunk, s_ref, lo=0):
        kblk = k_ref[0, key_slice(chunk), :]
        for c in range(2):
            s_ref[c, :, lo:] = jnp.dot(kblk, w_ref[c, :, lo:], preferred_element_type=F32)

    def softmax_pv(chunk, s_ref, carry, mask, lo=0):
        vblk = jnp.concatenate([vt_ref[0, :, key_slice(chunk)], ones_rows], axis=0)
        out = []
        for c in range(2):
            m = carry[c][:, lo:]
            s = s_ref[c, :, lo:]
            if mask is not None:
                s = jnp.where(mask[:, lo:], s, NEG)
            m_new = jnp.maximum(m, jnp.max(s, axis=0, keepdims=True))
            alpha = jnp.exp2(m - m_new)
            p = jnp.exp2((s - m_new).astype(BF16))
            pv = jnp.dot(vblk, p, preferred_element_type=F32)
            acc_ref[c, :, lo:] = alpha * acc_ref[c, :, lo:] + pv[0:ACC_ROWS]
            out.append(m_new if lo == 0 else jnp.concatenate([carry[c][:, :lo], m_new], axis=1))
        return tuple(out)

    init = jnp.full((1, tq), NEG, F32)
    scores(0, sa_ref)

    def pair(j, carry):
        scores(2 * j + 1, sb_ref)
        carry = softmax_pv(2 * j, sa_ref, carry, None)
        scores(2 * j + 2, sa_ref)
        return softmax_pv(2 * j + 1, sb_ref, carry, None)

    carry = lax.fori_loop(0, qi * (per_q // 2), pair, (init, init))
    key_rel = lax.broadcasted_iota(jnp.int32, (tk, tq), 0)
    q_rel = lax.broadcasted_iota(jnp.int32, (tk, tq), 1)
    base = qi * per_q
    bufs = (sa_ref, sb_ref)
    for c in range(per_q):
        if c + 1 < per_q:
            scores(base + c + 1, bufs[(c + 1) % 2], lo=(c + 1) * tk)
        carry = softmax_pv(base + c, bufs[c % 2], carry, key_rel + c * tk <= q_rel, lo=c * tk)

    lam = _lam_value(lq1_ref[...], lk1_ref[...], lq2_ref[...], lk2_ref[...], lam_init)
    l1 = acc_ref[0, V_DIM:V_DIM + 1, :]
    l2 = acc_ref[1, V_DIM:V_DIM + 1, :]
    ot = acc_ref[0, 0:V_DIM, :] * (1.0 / l1) - lam * (acc_ref[1, 0:V_DIM, :] * (1.0 / l2))
    ot = ot * lax.rsqrt(jnp.mean(ot * ot, axis=0, keepdims=True) + EPS)
    o = ot.T * g_ref[...] * (1.0 - lam_init)
    o_ref[0] = o.astype(o_ref.dtype)


def _attn_prompt(qt, kbf, vt, lam_rows, subln_g, *, tq, tk, lam_init):
    b, aw, s = qt.shape
    assert tq % (2 * tk) == 0
    kern = functools.partial(_attn_prompt_kernel, tq=tq, tk=tk, lam_init=lam_init)
    small = _const_spec((1, D_HEAD))
    return pl.pallas_call(
        kern,
        out_shape=jax.ShapeDtypeStruct((b, s, aw), BF16),
        grid=(b, N_HEADS, s // tq),
        in_specs=[
            pl.BlockSpec((1, QK_DIM, tq), lambda i, h, j: (i, h, j)),
            pl.BlockSpec((1, s, QK_DIM), lambda i, h, j: (i, 0, h)),
            pl.BlockSpec((1, V_DIM, s), lambda i, h, j: (i, h, 0)),
            small, small, small, small,
            _const_spec((1, V_DIM)),
        ],
        out_specs=pl.BlockSpec((1, tq, V_DIM), lambda i, h, j: (i, j, h)),
        scratch_shapes=[pltpu.VMEM((2, QK_DIM, tq), BF16),
                        pltpu.VMEM((2, tk, tq), F32), pltpu.VMEM((2, tk, tq), F32),
                        pltpu.VMEM((2, ACC_ROWS, tq), F32)],
        compiler_params=pltpu.CompilerParams(
            dimension_semantics=("arbitrary", "arbitrary", "arbitrary"),
            vmem_limit_bytes=VMEM_LIMIT_BYTES),
        name="attn_prompt",
    )(qt, kbf, vt, *lam_rows, subln_g)


def _post_kernel(ya_ref, yc_ref, x_ref, mods_ref, gpm_ref, gpf_ref, gpo_ref,
                 wout_ref, wg_ref, wu_ref, wd_ref, y_ref, acc_ref, *, d, aw, ff_chunks):
    gate1 = mods_ref[0, :, 2 * d:3 * d]
    shift2 = mods_ref[0, :, 3 * d:4 * d]
    scale2 = mods_ref[0, :, 4 * d:5 * d]
    gate2 = mods_ref[0, :, 5 * d:6 * d]
    m = jnp.dot(ya_ref[0].astype(BF16), wout_ref[0:aw, :], preferred_element_type=F32)
    m = m + jnp.dot(yc_ref[0].astype(BF16), wout_ref[aw:, :], preferred_element_type=F32)
    x1 = x_ref[0] + gate1 * _rms(m, gpm_ref[...])
    hb = (_rms(x1, gpf_ref[...]) * (1.0 + scale2) + shift2).astype(BF16)
    for n, (c0, c1) in enumerate(ff_chunks):
        g = jnp.dot(hb, wg_ref[:, c0:c1], preferred_element_type=F32)
        u = jnp.dot(hb, wu_ref[:, c0:c1], preferred_element_type=F32)
        a = (_silu(g) * u).astype(BF16)
        part = jnp.dot(a, wd_ref[c0:c1, :], preferred_element_type=F32)
        if n == 0:
            acc_ref[...] = part
        else:
            acc_ref[...] += part
    y_ref[0] = x1 + gate2 * _rms(acc_ref[...], gpo_ref[...])


def _post(ya, yc, x, mods, g_post_mix, g_pre_ffn, g_post_ffn, w_out, w_gate, w_up, w_down, *, tm):
    b, s, d = x.shape
    aw = ya.shape[-1]
    cw = yc.shape[-1]
    dff = w_gate.shape[1]
    step = 512
    ff_chunks = tuple((c, min(c + step, dff)) for c in range(0, dff, step))
    mod_rows = mods.shape[1]
    mt = 1 if mod_rows == 1 else tm
    kern = functools.partial(_post_kernel, d=d, aw=aw, ff_chunks=ff_chunks)
    return pl.pallas_call(
        kern,
        out_shape=jax.ShapeDtypeStruct((b, s, d), F32),
        grid=(b, s // tm),
        in_specs=[
            pl.BlockSpec((1, tm, aw), lambda i, j: (i, j, 0)),
            pl.BlockSpec((1, tm, cw), lambda i, j: (i, j, 0)),
            pl.BlockSpec((1, tm, d), lambda i, j: (i, j, 0)),
            pl.BlockSpec((1, mt, N_ADA * d),
                         (lambda i, j: (i, 0, 0)) if mod_rows == 1 else (lambda i, j: (i, j, 0))),
            _const_spec((1, d)), _const_spec((1, d)), _const_spec((1, d)),
            _const_spec(w_out.shape), _const_spec(w_gate.shape),
            _const_spec(w_up.shape), _const_spec(w_down.shape),
        ],
        out_specs=pl.BlockSpec((1, tm, d), lambda i, j: (i, j, 0)),
        scratch_shapes=[pltpu.VMEM((tm, d), F32)],
        compiler_params=pltpu.CompilerParams(
            dimension_semantics=("arbitrary", "arbitrary"),
            vmem_limit_bytes=VMEM_LIMIT_BYTES),
        name="post_ffn",
    )(ya, yc, x, mods, g_post_mix, g_pre_ffn, g_post_ffn, w_out, w_gate, w_up, w_down)


def _front_sample_kernel(x_ref, mods_ref, g_ref, win_ref, bglu_ref, st_ref, cw_ref, cb_ref,
                         lg_ref, lb_ref, q_ref, k_ref, v_ref, u_ref, yc_ref, *, d, aw, cw):
    x = x_ref[...]
    shift = mods_ref[:, 0:d]
    scale = mods_ref[:, d:2 * d]
    hb = (_rms(x, g_ref[...]) * (1.0 + scale) + shift).astype(BF16)
    z = jnp.dot(hb, win_ref[...], preferred_element_type=F32)
    q_ref[...] = z[:, 0:aw] * Q_SCALE
    k_ref[...] = z[:, aw:2 * aw]
    v_ref[...] = z[:, 2 * aw:3 * aw]
    ga = z[:, 3 * aw:] + bglu_ref[...]
    u = ga[:, 0:cw] * jax.nn.sigmoid(ga[:, cw:])
    u_ref[...] = u
    y = cw_ref[CONV_STATE:CONV_K, :] * u + cb_ref[...]
    for j in range(CONV_STATE):
        y = y + cw_ref[j:j + 1, :] * st_ref[j]
    mu = jnp.mean(y, axis=-1, keepdims=True)
    yc = y - mu
    yn = yc * lax.rsqrt(jnp.mean(yc * yc, axis=-1, keepdims=True) + EPS)
    yc_ref[...] = _silu(yn * lg_ref[...] + lb_ref[...]).astype(yc_ref.dtype)


def _front_sample(x, mods, g_pre, w_in, b_glu, state_t, conv_w, conv_b, ln_g, ln_b):
    rows, d = x.shape
    cw = conv_w.shape[1]
    aw = (w_in.shape[1] - 2 * cw) // 3
    kern = functools.partial(_front_sample_kernel, d=d, aw=aw, cw=cw)
    full = lambda a: pl.BlockSpec(a.shape, lambda i: (0,) * a.ndim)
    args = (x, mods, g_pre, w_in, b_glu, state_t, conv_w, conv_b, ln_g, ln_b)
    out_shapes = (
        jax.ShapeDtypeStruct((rows, aw), F32),
        jax.ShapeDtypeStruct((rows, aw), F32),
        jax.ShapeDtypeStruct((rows, aw), F32),
        jax.ShapeDtypeStruct((rows, cw), F32),
        jax.ShapeDtypeStruct((rows, cw), BF16),
    )
    return pl.pallas_call(
        kern,
        out_shape=out_shapes,
        grid=(1,),
        in_specs=[full(a) for a in args],
        out_specs=tuple(pl.BlockSpec(o.shape, lambda i: (0, 0)) for o in out_shapes),
        compiler_params=pltpu.CompilerParams(
            dimension_semantics=("arbitrary",), vmem_limit_bytes=VMEM_LIMIT_BYTES),
        name="front_sample",
    )(*args)


ROWS = 16
OUT_ROWS = 8
GROUPS = 2
N_SLOTS = 3


def _head_rows(x_row, sub, lane):
    seg_of_lane = lax.shift_right_logical(lane, 7)
    full = jnp.where(seg_of_lane == jnp.bitwise_and(sub, N_HEADS - 1), x_row, 0.0)
    out = full[:, 0:128]
    for h in range(1, N_HEADS):
        out = out + full[:, h * 128:(h + 1) * 128]
    return out


def _attn_sample_kernel(pt_ref, q_ref, kn_ref, vn_ref, lq1_ref, lk1_ref, lq2_ref, lk2_ref, g_ref,
                        ck_ref, cv_ref, o_ref, kbuf_ref, vbuf_ref, sem_ref, m_ref, l_ref, acc_ref,
                        *, pp, n_steps, total_steps, lam_init):
    b = pl.program_id(0)
    j = pl.program_id(1)
    t = b * n_steps + j

    def page_copies(step, slot):
        seq = step // n_steps
        first = (step - seq * n_steps) * pp
        copies = []
        for i in range(pp):
            page = pt_ref[seq, first + i]
            copies.append(pltpu.make_async_copy(ck_ref.at[page], kbuf_ref.at[slot, i], sem_ref.at[0, slot]))
            copies.append(pltpu.make_async_copy(cv_ref.at[page], vbuf_ref.at[slot, i], sem_ref.at[1, slot]))
        return copies

    @pl.when(t == 0)
    def _():
        for s in range(N_SLOTS - 1):
            for c in page_copies(s, s):
                c.start()

    ahead = t + (N_SLOTS - 1)

    @pl.when(ahead < total_steps)
    def _():
        for c in page_copies(ahead, lax.rem(ahead, N_SLOTS)):
            c.start()

    slot = lax.rem(t, N_SLOTS)
    for c in page_copies(t, slot):
        c.wait()

    aw = q_ref.shape[-1]
    sub_w = lax.broadcasted_iota(jnp.int32, (ROWS, aw), 0)
    lane_w = lax.broadcasted_iota(jnp.int32, (ROWS, aw), 1)
    sub = lax.broadcasted_iota(jnp.int32, (ROWS, QK_DIM), 0)
    lane = lax.broadcasted_iota(jnp.int32, (ROWS, QK_DIM), 1)
    own_map = lax.shift_right_logical(lane, 6) == lax.shift_right_logical(sub, 2)
    qq = jnp.where(own_map, _head_rows(q_ref[0], sub_w, lane_w), 0.0)
    qq_bf = qq.astype(BF16)

    @pl.when(j == 0)
    def _():
        m_ref[...] = jnp.full(m_ref.shape, NEG, F32)
        l_ref[...] = jnp.zeros(l_ref.shape, F32)
        acc_ref[...] = jnp.zeros(acc_ref.shape, F32)

    n_rows = kbuf_ref.shape[2]
    col = lax.broadcasted_iota(jnp.int32, (ROWS, n_rows), 1)
    srow = lax.broadcasted_iota(jnp.int32, (ROWS, n_rows), 0)
    same_head = jnp.bitwise_and(col, N_HEADS - 1) == jnp.bitwise_and(srow, N_HEADS - 1)
    per_group = pp // GROUPS
    for g in range(GROUPS):
        pages = range(g * per_group, (g + 1) * per_group)
        parts = []
        for i in pages:
            s_i = lax.dot_general(qq_bf, kbuf_ref[slot, i].astype(BF16), (((1,), (1,)), ((), ())),
                                  preferred_element_type=F32)
            parts.append(jnp.where(same_head, s_i, NEG))
        s = jnp.concatenate(parts, axis=1)
        m_old = m_ref[g]
        m_new = jnp.maximum(m_old, jnp.max(s, axis=1, keepdims=True))
        alpha = jnp.exp2(m_old - m_new)
        p = jnp.exp2(s - m_new)
        l_ref[g] = alpha * l_ref[g] + jnp.sum(p, axis=1, keepdims=True)
        m_ref[g] = m_new
        pv = jnp.zeros((ROWS, V_DIM), F32)
        for n, i in enumerate(pages):
            pv = pv + jnp.dot(p[:, n * n_rows:(n + 1) * n_rows].astype(BF16),
                              vbuf_ref[slot, i].astype(BF16), preferred_element_type=F32)
        acc_ref[g] = alpha * acc_ref[g] + pv

    @pl.when(j == n_steps - 1)
    def _():
        k16 = _head_rows(kn_ref[0], sub_w, lane_w)
        v16 = _head_rows(vn_ref[0], sub_w, lane_w)
        s_new = jnp.sum(qq * k16, axis=1, keepdims=True)
        m_fin = s_new
        for g in range(GROUPS):
            m_fin = jnp.maximum(m_fin, m_ref[g])
        p_new = jnp.exp2(s_new - m_fin)
        l_fin = p_new
        acc = p_new * v16
        for g in range(GROUPS):
            alpha = jnp.exp2(m_ref[g] - m_fin)
            l_fin = l_fin + alpha * l_ref[g]
            acc = acc + alpha * acc_ref[g]
        o = acc * (1.0 / l_fin)
        lam = _lam_value(lq1_ref[...], lk1_ref[...], lq2_ref[...], lk2_ref[...], lam_init)
        oc = o - lam * pltpu.roll(o, ROWS - N_HEADS, axis=0)
        y = oc * lax.rsqrt(jnp.mean(oc * oc, axis=-1, keepdims=True) + EPS)
        y = y * g_ref[...] * (1.0 - lam_init)
        o_ref[0] = y[0:OUT_ROWS]


def _attn_sample(page_table, q, k_new, v_new, lam_rows, subln_g, cache_k, cache_v, *, pp, lam_init):
    db, n_pages = page_table.shape
    n_rows = cache_k.shape[1]
    aw = q.shape[-1]
    n_steps = n_pages // pp
    total_steps = db * n_steps
    assert pp % GROUPS == 0 and total_steps >= N_SLOTS
    kern = functools.partial(_attn_sample_kernel, pp=pp, n_steps=n_steps, total_steps=total_steps,
                             lam_init=lam_init)
    row = pl.BlockSpec((1, 1, aw), lambda b, j, pt: (b, 0, 0))
    small = pl.BlockSpec((1, D_HEAD), lambda b, j, pt: (0, 0))
    hbm = pl.BlockSpec(memory_space=pl.ANY)

    grid_spec = pltpu.PrefetchScalarGridSpec(
        num_scalar_prefetch=1,
        grid=(db, n_steps),
        in_specs=[row, row, row, small, small, small, small,
                  pl.BlockSpec((1, V_DIM), lambda b, j, pt: (0, 0)), hbm, hbm],
        out_specs=pl.BlockSpec((1, OUT_ROWS, V_DIM), lambda b, j, pt: (b, 0, 0)),
        scratch_shapes=[pltpu.VMEM((N_SLOTS, pp, n_rows, QK_DIM), F32),
                        pltpu.VMEM((N_SLOTS, pp, n_rows, V_DIM), F32),
                        pltpu.SemaphoreType.DMA((2, N_SLOTS)),
                        pltpu.VMEM((GROUPS, ROWS, 1), F32), pltpu.VMEM((GROUPS, ROWS, 1), F32),
                        pltpu.VMEM((GROUPS, ROWS, V_DIM), F32)],
    )
    out = pl.pallas_call(
        kern,
        out_shape=jax.ShapeDtypeStruct((db, OUT_ROWS, V_DIM), F32),
        grid_spec=grid_spec,
        compiler_params=pltpu.CompilerParams(
            dimension_semantics=("arbitrary", "arbitrary"),
            vmem_limit_bytes=VMEM_LIMIT_BYTES),
        name="attn_sample",
    )(page_table, q.reshape(db, 1, aw), k_new.reshape(db, 1, aw), v_new.reshape(db, 1, aw),
      *lam_rows, subln_g, cache_k, cache_v)
    return out[:, 0:N_HEADS, :].reshape(db, N_HEADS * V_DIM)


def _tile(n, pref):
    t = min(n, pref)
    while n % t:
        t //= 2
    return t


def kernel(x_prompt, x_sample, c_prompt, c_sample, cache_k, cache_v, state_conv, page_table, w_ada, b_ada, g_pre_mix, g_post_mix, g_pre_ffn, g_post_ffn, w_in, b_glu, conv_w, conv_b, conv_ln_g, conv_ln_b, lambda_q1, lambda_k1, lambda_q2, lambda_k2, subln_g, w_out, w_gate, w_up, w_down):
    depth = w_in.shape[0]
    assert depth == 1, "single-layer step"
    l = 0
    lam_init = 0.8 - 0.6 * math.exp(-0.3 * l)
    b, s, d = x_prompt.shape
    db = x_sample.shape[0]
    assert x_sample.shape[1] == 1
    cw = conv_w.shape[-1]
    aw = N_HEADS * V_DIM
    n_pool, page = cache_k.shape[1], cache_k.shape[2]

    row = lambda a: a[l].reshape(1, -1)
    lam_rows = (row(lambda_q1), row(lambda_k1), row(lambda_q2), row(lambda_k2))
    g_sub = row(subln_g)

    mods = _ada(jnp.concatenate([c_prompt, c_sample], axis=0), w_ada[l], b_ada[l])
    mods_p = mods[:b].reshape(b, 1, N_ADA * d)
    mods_s = mods[b:]

    w_in_bf = w_in[l].astype(BF16)
    w_kv = w_in_bf[:, aw:3 * aw]
    w_glu = w_in_bf[:, 3 * aw:]
    w_qvt = jnp.concatenate([w_in_bf[:, 0:aw], w_in_bf[:, 2 * aw:3 * aw]], axis=1).T
    w_out_bf = w_out[l].astype(BF16)
    w_gate_bf = w_gate[l].astype(BF16)
    w_up_bf = w_up[l].astype(BF16)
    w_down_bf = w_down[l].astype(BF16)

    state = state_conv[l]
    q_s, k_s, v_s, u_s, yc_s = _front_sample(
        x_sample.reshape(db, d), mods_s, row(g_pre_mix), w_in_bf, row(b_glu),
        jnp.transpose(state, (1, 0, 2)), conv_w[l], row(conv_b), row(conv_ln_g), row(conv_ln_b))
    pp = _tile(page_table.shape[1], 16)
    ya_s = _attn_sample(page_table + l * n_pool, q_s, k_s, v_s, lam_rows, g_sub,
                        cache_k.reshape(depth * n_pool, page * N_HEADS, QK_DIM),
                        cache_v.reshape(depth * n_pool, page * N_HEADS, V_DIM),
                        pp=pp, lam_init=lam_init)
    y_sample = _post(ya_s.reshape(1, db, aw), yc_s.reshape(1, db, cw), x_sample.reshape(1, db, d),
                     mods_s.reshape(1, db, N_ADA * d), row(g_post_mix), row(g_pre_ffn),
                     row(g_post_ffn), w_out_bf, w_gate_bf, w_up_bf, w_down_bf, tm=_tile(db, 512))
    conv_s = jnp.concatenate([state[:, 1:, :], u_s[:, None, :]], axis=1)

    tm = _tile(s, 512)
    k_p, v_p, kbf, qt, vt, yc_p, u_tail = _front_prompt(
        x_prompt, mods_p, row(g_pre_mix), w_glu, w_kv, w_qvt, row(b_glu),
        conv_w[l], row(conv_b), row(conv_ln_g), row(conv_ln_b), tm=tm)
    tq = _tile(s, 4096)
    ya_p = _attn_prompt(qt, kbf, vt, lam_rows, g_sub, tq=tq, tk=tq // 16, lam_init=lam_init)
    y_prompt = _post(ya_p, yc_p, x_prompt, mods_p, row(g_post_mix), row(g_pre_ffn), row(g_post_ffn),
                     w_out_bf, w_gate_bf, w_up_bf, w_down_bf, tm=tm)
    conv_p = u_tail[:, HALO - CONV_STATE:, :]

    return (y_prompt,
            y_sample.reshape(db, 1, d),
            k_p.reshape(1, b, s, N_HEADS, QK_DIM),
            v_p.reshape(1, b, s, N_HEADS, V_DIM),
            conv_p[None],
            k_s.reshape(1, db, 1, N_HEADS, QK_DIM),
            v_s.reshape(1, db, 1, N_HEADS, V_DIM),
            conv_s[None])
```
